```python
import jax, jax.numpy as jnp
from jax import lax
import numpy as np

D_MODEL = 1024
BATCH = 2
SEQ = 8192
DEPTH = 1

EXPAND = 2
D_INNER = EXPAND * D_MODEL
SSD_WIDTH = D_INNER // 2
ATT_WIDTH = D_INNER - SSD_WIDTH
SSD_HEAD_DIM = 64
SSD_HEADS = SSD_WIDTH // SSD_HEAD_DIM
N_GROUPS = 2
HEADS_PER_GROUP = SSD_HEADS // N_GROUPS
D_STATE = 128
CONV_WIDTH = 4
CHUNK = 128
ATT_HEAD_DIM = 64
ATT_HEADS = ATT_WIDTH // ATT_HEAD_DIM
Q_BLOCK = 128
PLE_DIM = 256
EPS = 1e-6
CONV_CH = SSD_WIDTH + 2 * N_GROUPS * D_STATE
IN_SPLITS = (SSD_WIDTH, CONV_CH, SSD_HEADS, ATT_WIDTH, ATT_WIDTH, ATT_WIDTH, ATT_WIDTH, ATT_HEADS)
IN_COLS = sum(IN_SPLITS)

kernel_name = "hymba_ssd_fox_hybrid_layer"


def _split_points(sizes):
    return [int(v) for v in np.cumsum(sizes)[:-1]]


def rms_norm(x, g):
    xf = x.astype(jnp.float32)
    y = xf * lax.rsqrt(jnp.mean(xf * xf, axis=-1, keepdims=True) + EPS)
    return (y * g.astype(jnp.float32)).astype(x.dtype)


def causal_depthwise_conv(u, w, b):
    out = lax.conv_general_dilated(
        u, w[:, None, :].astype(u.dtype), window_strides=(1,),
        padding=((CONV_WIDTH - 1, 0),), dimension_numbers=('NWC', 'WIO', 'NWC'),
        feature_group_count=u.shape[-1])
    return out + b.astype(u.dtype)


def ssd_scan(x, dt, a, b_mat, c_mat):
    bsz, seqlen = x.shape[0], x.shape[1]
    nc = seqlen // CHUNK
    f32 = jnp.float32
    xdt = (x.astype(f32) * dt[..., None]).reshape(
        bsz, nc, CHUNK, N_GROUPS, HEADS_PER_GROUP, SSD_HEAD_DIM)
    a_dt = (dt * a).reshape(bsz, nc, CHUNK, N_GROUPS, HEADS_PER_GROUP)
    a_cs = jnp.cumsum(jnp.moveaxis(a_dt, 2, -1), axis=-1)
    bm = b_mat.astype(f32).reshape(bsz, nc, CHUNK, N_GROUPS, D_STATE)
    cm = c_mat.astype(f32).reshape(bsz, nc, CHUNK, N_GROUPS, D_STATE)
    idx = jnp.arange(CHUNK)
    causal = idx[:, None] >= idx[None, :]
    seg = a_cs[..., :, None] - a_cs[..., None, :]
    decay = jnp.exp(jnp.where(causal, seg, -jnp.inf))
    cb = jnp.einsum('bclgn,bcsgn->bcgls', cm, bm)
    y_diag = jnp.einsum('bcgls,bcgrls,bcsgrp->bclgrp', cb, decay, xdt)
    decay_to_end = jnp.exp(a_cs[..., -1:] - a_cs)
    chunk_states = jnp.einsum('bclgn,bcgrl,bclgrp->bcgrpn', bm, decay_to_end, xdt)
    chunk_decay = jnp.exp(a_cs[..., -1])

    def step(h, inp):
        s_c, d_c = inp
        return h * d_c[..., None, None] + s_c, h

    h0 = jnp.zeros((bsz, N_GROUPS, HEADS_PER_GROUP, SSD_HEAD_DIM, D_STATE), f32)
    _, prev = lax.scan(step, h0, (jnp.moveaxis(chunk_states, 1, 0),
                                  jnp.moveaxis(chunk_decay, 1, 0)))
    prev = jnp.moveaxis(prev, 0, 1)
    y_off = jnp.einsum('bclgn,bcgrpn,bcgrl->bclgrp', cm, prev, jnp.exp(a_cs))
    return (y_diag + y_off).reshape(bsz, seqlen, SSD_HEADS, SSD_HEAD_DIM)


def forgetting_attention(q, k, v, log_f):
    bsz, seqlen = q.shape[0], q.shape[1]
    nblk = seqlen // Q_BLOCK
    cum = jnp.moveaxis(jnp.cumsum(log_f, axis=1), 1, 2)
    qh = jnp.moveaxis(q, 1, 2)
    kh = jnp.moveaxis(k, 1, 2)
    vh = jnp.moveaxis(v, 1, 2)
    scale = ATT_HEAD_DIM ** -0.5
    q_blocks = qh.reshape(bsz, ATT_HEADS, nblk, Q_BLOCK, ATT_HEAD_DIM).transpose(2, 0, 1, 3, 4)
    cq_blocks = cum.reshape(bsz, ATT_HEADS, nblk, Q_BLOCK).transpose(2, 0, 1, 3)
    kpos = jnp.arange(seqlen)
    starts = jnp.arange(nblk) * Q_BLOCK

    def block(args):
        qb, cqb, start = args
        s = jnp.einsum('bhqd,bhkd->bhqk', qb, kh,
                       preferred_element_type=jnp.float32) * scale
        s = s + cqb[..., None] - cum[:, :, None, :]
        qpos = start + jnp.arange(Q_BLOCK)
        s = jnp.where(qpos[:, None] >= kpos[None, :], s, -jnp.inf)
        pr = jax.nn.softmax(s, axis=-1)
        return jnp.einsum('bhqk,bhkd->bhqd', pr.astype(vh.dtype), vh)

    out = lax.map(block, (q_blocks, cq_blocks, starts))
    return out.transpose(1, 0, 3, 2, 4).reshape(bsz, seqlen, ATT_HEADS, ATT_HEAD_DIM)


def setup_inputs(seed: int = 0) -> dict:
    key = jax.random.key(seed)
    ks = jax.random.split(key, 20)
    f32 = jnp.float32
    nrm = lambda k, shape, s: (jax.random.normal(k, shape, f32) * s)
    x = jax.random.normal(ks[0], (BATCH, SEQ, D_MODEL), f32)
    p = jax.random.normal(ks[1], (DEPTH, BATCH, SEQ, PLE_DIM), f32)
    norm_g = 1.0 + nrm(ks[2], (DEPTH, D_MODEL), 0.02)
    w_in = nrm(ks[3], (DEPTH, D_MODEL, IN_COLS), D_MODEL ** -0.5)
    conv_w = nrm(ks[4], (DEPTH, CONV_WIDTH, CONV_CH), CONV_WIDTH ** -0.5)
    conv_b = nrm(ks[5], (DEPTH, CONV_CH), 0.02)
    dt0 = jnp.exp(jax.random.uniform(ks[6], (DEPTH, SSD_HEADS), f32,
                                     jnp.log(1e-3), jnp.log(1e-1)))
    dt_bias = dt0 + jnp.log(-jnp.expm1(-dt0))
    a_log = jnp.log(jax.random.uniform(ks[7], (DEPTH, SSD_HEADS), f32, 1.0, 16.0))
    d_skip = 1.0 + nrm(ks[8], (DEPTH, SSD_HEADS), 0.02)
    ssd_norm_g = 1.0 + nrm(ks[9], (DEPTH, SSD_WIDTH), 0.02)
    fg_bias = jax.random.uniform(ks[10], (DEPTH, ATT_HEADS), f32, 1.0, 5.0)
    att_norm_g = 1.0 + nrm(ks[11], (DEPTH, ATT_HEAD_DIM), 0.02)
    w_out = nrm(ks[12], (DEPTH, D_INNER, D_MODEL), D_INNER ** -0.5)
    ple_norm_g = 1.0 + nrm(ks[13], (DEPTH, D_MODEL), 0.02)
    w_ple_gate = nrm(ks[14], (DEPTH, D_MODEL, D_MODEL), D_MODEL ** -0.5)
    w_ple_proj = nrm(ks[15], (DEPTH, PLE_DIM, D_MODEL), PLE_DIM ** -0.5)
    final_norm_g = 1.0 + nrm(ks[16], (D_MODEL,), 0.02)
    return {"x": x, "p": p, "norm_g": norm_g, "w_in": w_in, "conv_w": conv_w,
            "conv_b": conv_b, "dt_bias": dt_bias, "a_log": a_log, "d_skip": d_skip,
            "ssd_norm_g": ssd_norm_g, "fg_bias": fg_bias, "att_norm_g": att_norm_g,
            "w_out": w_out, "ple_norm_g": ple_norm_g, "w_ple_gate": w_ple_gate,
            "w_ple_proj": w_ple_proj, "final_norm_g": final_norm_g}


def reference(x, p, norm_g, w_in, conv_w, conv_b, dt_bias, a_log, d_skip, ssd_norm_g,
              fg_bias, att_norm_g, w_out, ple_norm_g, w_ple_gate, w_ple_proj, final_norm_g):
    f32 = jnp.float32
    bsz, seqlen = x.shape[0], x.shape[1]
    in_pts = _split_points(IN_SPLITS)
    h = x
    for i in range(DEPTH):
        u = rms_norm(h, norm_g[i])
        proj = u @ w_in[i]
        z_ssd, xbc, dt_raw, z_att, q, k, v, f_raw = jnp.split(proj, in_pts, axis=-1)

        xbc = jax.nn.silu(causal_depthwise_conv(xbc, conv_w[i], conv_b[i]))
        xs, bm, cm = jnp.split(xbc, [SSD_WIDTH, SSD_WIDTH + N_GROUPS * D_STATE], axis=-1)
        xs = xs.reshape(bsz, seqlen, SSD_HEADS, SSD_HEAD_DIM)
        bm = bm.reshape(bsz, seqlen, N_GROUPS, D_STATE)
        cm = cm.reshape(bsz, seqlen, N_GROUPS, D_STATE)
        dt = jax.nn.softplus(dt_raw.astype(f32) + dt_bias[i].astype(f32))
        a = -jnp.exp(a_log[i].astype(f32))
        y = ssd_scan(xs, dt, a, bm, cm)
        y = y + d_skip[i].astype(f32)[:, None] * xs.astype(f32)
        y = y.reshape(bsz, seqlen, SSD_WIDTH).astype(x.dtype) * jax.nn.silu(z_ssd)
        y = rms_norm(y.reshape(bsz, seqlen, N_GROUPS, SSD_WIDTH // N_GROUPS),
                     ssd_norm_g[i].reshape(N_GROUPS, SSD_WIDTH // N_GROUPS))
        y_ssd = y.reshape(bsz, seqlen, SSD_WIDTH)

        log_f = jax.nn.log_sigmoid(f_raw.astype(f32) + fg_bias[i].astype(f32))
        hs = (bsz, seqlen, ATT_HEADS, ATT_HEAD_DIM)
        att = forgetting_attention(q.reshape(hs), k.reshape(hs), v.reshape(hs), log_f)
        att = rms_norm(att, att_norm_g[i])
        y_att = att.reshape(bsz, seqlen, ATT_WIDTH) * jax.nn.silu(z_att)

        h = h + jnp.concatenate([y_ssd, y_att], axis=-1) @ w_out[i]

        gate = jax.nn.sigmoid(rms_norm(h, ple_norm_g[i]) @ w_ple_gate[i])
        h = h + gate * (p[i].astype(h.dtype) @ w_ple_proj[i])
    return rms_norm(h, final_norm_g)
```

```python
import functools
import math

import jax
import jax.numpy as jnp
import numpy as np
from jax import lax
from jax.experimental import pallas as pl
from jax.experimental.pallas import tpu as pltpu

F32 = jnp.float32
BF16 = jnp.bfloat16

D_MODEL = 1024
SSD_WIDTH = 1024
ATT_WIDTH = 1024
SSD_HEAD_DIM = 64
SSD_HEADS = 16
N_GROUPS = 2
HEADS_PER_GROUP = SSD_HEADS // N_GROUPS
GROUP_WIDTH = SSD_WIDTH // N_GROUPS
D_STATE = 128
CONV_WIDTH = 4
CHUNK = 128
ATT_HEAD_DIM = 64
ATT_HEADS = 16
PLE_DIM = 256
EPS = 1e-6
CONV_CH = SSD_WIDTH + 2 * N_GROUPS * D_STATE

LANES = 128
SUBLANES = 8
HALF = LANES // 2

LOG2E = 1.4426950408889634
NEG_BIG = -1e30

SEC_ZS = (0, SSD_WIDTH)
SEC_XBC = (SEC_ZS[0] + SEC_ZS[1], CONV_CH)
SEC_ZA = (SEC_XBC[0] + SEC_XBC[1], ATT_WIDTH)
SEC_Q = (SEC_ZA[0] + SEC_ZA[1], ATT_WIDTH)
SEC_K = (SEC_Q[0] + SEC_Q[1], ATT_WIDTH)
SEC_V = (SEC_K[0] + SEC_K[1], ATT_WIDTH)
MAIN_COLS = SEC_V[0] + SEC_V[1]
TAIL_DT = 0
TAIL_F = SSD_HEADS

IN_ROWS = 512
IN_COL_CHUNK = 512
PREP_ROWS = 512
SSD_ROWS = 256
ATT_TQ = 512
ATT_TK = 512
OUT_ROWS = 512
VMEM_LIMIT = 56 * 1024 * 1024


def _split_bf16(x, parts):
    out = []
    r = x
    for i in range(parts):
        h = r.astype(BF16)
        out.append(h)
        if i + 1 < parts:
            r = r - h.astype(F32)
    return out


def _softplus(x):
    return jnp.maximum(x, 0.0) + jnp.log1p(jnp.exp(-jnp.abs(x)))


def _silu(x):
    return x * jax.nn.sigmoid(x)


def _in_proj_kernel(x_ref, g_ref, wm_ref, wt_ref,
                    zs_ref, xbc_ref, za_ref, q_ref, k_ref, v_ref, tail_ref):
    x = x_ref[...]
    ms = jnp.mean(x * x, axis=-1, keepdims=True)
    u = (x * lax.rsqrt(ms + EPS) * g_ref[...]).astype(BF16)

    def section(out_ref, sec, mul=None):
        off, width = sec
        for c in range(0, width, IN_COL_CHUNK):
            acc = jnp.dot(u, wm_ref[:, off + c:off + c + IN_COL_CHUNK],
                          preferred_element_type=F32)
            if mul is not None:
                acc = acc * mul
            out_ref[:, c:c + IN_COL_CHUNK] = acc.astype(out_ref.dtype)

    section(zs_ref, SEC_ZS)
    section(xbc_ref, SEC_XBC)
    section(za_ref, SEC_ZA)
    section(q_ref, SEC_Q, mul=ATT_HEAD_DIM ** -0.5 * LOG2E)
    section(k_ref, SEC_K)
    section(v_ref, SEC_V)
    tail_ref[...] = jnp.dot(u, wt_ref[...], preferred_element_type=F32)


def _in_proj(x2, g, w_main, w_tail):
    n = x2.shape[0]
    row = lambda w: pl.BlockSpec((IN_ROWS, w), lambda i: (i, 0))
    full = lambda a: pl.BlockSpec(a.shape, lambda i: (0, 0))
    out_shapes = (
        jax.ShapeDtypeStruct((n, SSD_WIDTH), BF16),
        jax.ShapeDtypeStruct((n, CONV_CH), BF16),
        jax.ShapeDtypeStruct((n, ATT_WIDTH), BF16),
        jax.ShapeDtypeStruct((n, ATT_WIDTH), BF16),
        jax.ShapeDtypeStruct((n, ATT_WIDTH), BF16),
        jax.ShapeDtypeStruct((n, ATT_WIDTH), BF16),
        jax.ShapeDtypeStruct((n, LANES), F32),
    )
    return pl.pallas_call(
        _in_proj_kernel,
        out_shape=out_shapes,
        grid=(n // IN_ROWS,),
        in_specs=[row(D_MODEL), full(g), full(w_main), full(w_tail)],
        out_specs=(row(SSD_WIDTH), row(CONV_CH), row(ATT_WIDTH), row(ATT_WIDTH),
                   row(ATT_WIDTH), row(ATT_WIDTH), row(LANES)),
        compiler_params=pltpu.CompilerParams(
            dimension_semantics=("arbitrary",), vmem_limit_bytes=VMEM_LIMIT),
        name="in_proj",
    )(x2, g, w_main, w_tail)


def _aug_lane(head, j):
    pair, odd = divmod(head, 2)
    return pair * LANES + (j if odd else HALF + j)


def _aug_matrices():
    eq = np.zeros((4 * LANES, ATT_WIDTH), np.float32)
    ek = np.zeros((4 * LANES, ATT_WIDTH), np.float32)
    for h in range(ATT_HEADS):
        src = TAIL_F + h
        for part in range(3):
            eq[part * LANES + src, _aug_lane(h, part)] = 1.0
            ek[part * LANES + src, _aug_lane(h, 3 + part)] = -1.0
            eq[3 * LANES, _aug_lane(h, 3 + part)] = 1.0
            ek[3 * LANES, _aug_lane(h, part)] = 1.0
    return jnp.asarray(eq, BF16), jnp.asarray(ek, BF16)


def _tri3():
    tri = np.tril(np.ones((CHUNK, CHUNK), np.float32))
    return jnp.asarray(np.concatenate([tri, tri, tri], axis=1), BF16)


def _cumsum_rows(tri3, v):
    parts = _split_bf16(v, 3)
    return jnp.dot(tri3, jnp.concatenate(parts, axis=0), preferred_element_type=F32)


def _fox_prep_kernel(tail_ref, q_ref, k_ref, v_ref, fgb_ref, tri3_ref, eq_ref, ek_ref,
                     qa_ref, ka_ref, va_ref, carry_ref):
    @pl.when(pl.program_id(1) == 0)
    def _():
        carry_ref[...] = jnp.zeros_like(carry_ref)

    tri3 = tri3_ref[...]
    lane = lax.broadcasted_iota(jnp.int32, (CHUNK, LANES), 1)
    low = lane < HALF
    one_even = jnp.where(lane == HALF, 1.0, 0.0).astype(BF16)
    one_odd = jnp.where(lane == 0, 1.0, 0.0).astype(BF16)
    ones = jnp.ones((CHUNK, LANES), BF16)
    carry = carry_ref[...]
    for c in range(PREP_ROWS // CHUNK):
        rows = pl.ds(c * CHUNK, CHUNK)
        xf = tail_ref[rows, :] + fgb_ref[...]
        log_f = -_softplus(-xf)
        cum = _cumsum_rows(tri3, log_f) + carry
        carry = cum[CHUNK - 1:CHUNK, :]
        parts = _split_bf16(cum * LOG2E, 3)
        lhs = jnp.concatenate(parts + [ones], axis=1)
        aq = jnp.dot(lhs, eq_ref[...], preferred_element_type=F32).astype(BF16)
        ak = jnp.dot(lhs, ek_ref[...], preferred_element_type=F32).astype(BF16)
        for pair in range(ATT_HEADS // 2):
            cols = slice(pair * LANES, (pair + 1) * LANES)
            q2 = q_ref[rows, cols]
            k2 = k_ref[rows, cols]
            v2 = v_ref[rows, cols]
            qa_ref[0, 2 * pair, rows, :] = jnp.where(low, q2, aq[:, cols])
            qa_ref[0, 2 * pair + 1, rows, :] = jnp.where(low, aq[:, cols], q2)
            ka_ref[0, 2 * pair, rows, :] = jnp.where(low, k2, ak[:, cols])
            ka_ref[0, 2 * pair + 1, rows, :] = jnp.where(low, ak[:, cols], k2)
            va_ref[0, 2 * pair, rows, :] = jnp.where(low, v2, one_even)
            va_ref[0, 2 * pair + 1, rows, :] = jnp.where(low, one_odd, v2)
    carry_ref[...] = carry


def _fox_prep(tail, q, k, v, fgb, bsz, seqlen):
    nt = seqlen // PREP_ROWS
    tri3 = _tri3()
    eq, ek = _aug_matrices()
    row = lambda w: pl.BlockSpec((PREP_ROWS, w), lambda b, t: (b * nt + t, 0))
    full = lambda a: pl.BlockSpec(a.shape, lambda b, t: (0, 0))
    head_major = pl.BlockSpec((1, ATT_HEADS, PREP_ROWS, LANES), lambda b, t: (b, 0, t, 0))
    aug_shape = jax.ShapeDtypeStruct((bsz, ATT_HEADS, seqlen, LANES), BF16)
    return pl.pallas_call(
        _fox_prep_kernel,
        out_shape=(aug_shape, aug_shape, aug_shape),
        grid=(bsz, nt),
        in_specs=[row(LANES), row(ATT_WIDTH), row(ATT_WIDTH), row(ATT_WIDTH),
                  full(fgb), full(tri3), full(eq), full(ek)],
        out_specs=(head_major, head_major, head_major),
        scratch_shapes=[pltpu.VMEM((1, LANES), F32)],
        compiler_params=pltpu.CompilerParams(
            dimension_semantics=("arbitrary", "arbitrary"), vmem_limit_bytes=VMEM_LIMIT),
        name="fox_prep",
    )(tail, q, k, v, fgb, tri3, eq, ek)


def _head_expand2():
    e = np.zeros((2 * LANES, SSD_WIDTH), np.float32)
    for h in range(SSD_HEADS):
        e[TAIL_DT + h, h * SSD_HEAD_DIM:(h + 1) * SSD_HEAD_DIM] = 1.0
        e[LANES + TAIL_DT + h, h * SSD_HEAD_DIM:(h + 1) * SSD_HEAD_DIM] = 1.0
    return jnp.asarray(e, BF16)


def _ssd_kernel(xbc_ref, tail_ref, z_ref, cw_ref, cb_ref, dtb_ref, a_ref, dfull_ref, g_ref,
                tri3_ref, e2_ref, y_ref, xpad_ref, state_ref):
    halo = SUBLANES

    @pl.when(pl.program_id(1) == 0)
    def _():
        xpad_ref[0:halo, :] = jnp.zeros((halo, CONV_CH), F32)
        state_ref[...] = jnp.zeros_like(state_ref)

    xpad_ref[halo:halo + SSD_ROWS, :] = xbc_ref[...].astype(F32)

    tri3 = tri3_ref[...]
    row_i = lax.broadcasted_iota(jnp.int32, (CHUNK, CHUNK), 0)
    col_i = lax.broadcasted_iota(jnp.int32, (CHUNK, CHUNK), 1)
    causal = row_i >= col_i
    low = col_i < HALF

    for c in range(SSD_ROWS // CHUNK):
        r0 = c * CHUNK
        conv = cb_ref[...]
        for j in range(CONV_WIDTH):
            start = halo + r0 - (CONV_WIDTH - 1) + j
            conv = conv + cw_ref[j:j + 1, :] * xpad_ref[start:start + CHUNK, :]
        xc = _silu(conv)
        xs = xc[:, :SSD_WIDTH]
        bm = xc[:, SSD_WIDTH:SSD_WIDTH + N_GROUPS * D_STATE].astype(BF16)
        cm = xc[:, SSD_WIDTH + N_GROUPS * D_STATE:].astype(BF16)

        dt = _softplus(tail_ref[r0:r0 + CHUNK, :] + dtb_ref[...])
        adt = dt * a_ref[...]
        acs = _cumsum_rows(tri3, adt)
        acs_t = acs.T
        last = acs[CHUNK - 1:CHUNK, :]
        w = dt * jnp.exp(last - acs)
        ea = jnp.exp(acs)
        cd = jnp.broadcast_to(jnp.exp(last), (SUBLANES, LANES))
        stack = jnp.concatenate([dt, w, ea, cd], axis=0)
        lhs = jnp.concatenate(_split_bf16(stack, 2), axis=1)
        spread = jnp.dot(lhs, e2_ref[...], preferred_element_type=F32)
        dt_full = spread[0:CHUNK]
        w_full = spread[CHUNK:2 * CHUNK]
        ea_full = spread[2 * CHUNK:3 * CHUNK]
        cd_full = spread[3 * CHUNK:3 * CHUNK + 1]

        xdt = (xs * dt_full).astype(BF16)
        wx = (xs * w_full).astype(BF16)

        y_blocks = []
        yoff_blocks = []
        for g in range(N_GROUPS):
            gs = slice(g * D_STATE, (g + 1) * D_STATE)
            cb_g = lax.dot_general(cm[:, gs], bm[:, gs], (((1,), (1,)), ((), ())),
                                   preferred_element_type=F32)
            for jp in range(HEADS_PER_GROUP // 2):
                pair = g * (HEADS_PER_GROUP // 2) + jp
                x_pair = xdt[:, pair * LANES:(pair + 1) * LANES]
                ys = []
                for r in (2 * pair, 2 * pair + 1):
                    seg = acs[:, r:r + 1] - acs_t[r:r + 1, :]
                    dec = jnp.exp(jnp.where(causal, seg, -jnp.inf))
                    m = (cb_g * dec).astype(BF16)
                    ys.append(jnp.dot(m, x_pair, preferred_element_type=F32))
                y_blocks.append(jnp.where(low, ys[0], ys[1]))
            hs = slice(g * GROUP_WIDTH, (g + 1) * GROUP_WIDTH)
            st = state_ref[:, hs]
            yoff_blocks.append(jnp.dot(cm[:, gs], st.astype(BF16), preferred_element_type=F32))
            s_g = lax.dot_general(bm[:, gs], wx[:, hs], (((0,), (0,)), ((), ())),
                                  preferred_element_type=F32)
            state_ref[:, hs] = st * cd_full[:, hs] + s_g
        y = jnp.concatenate(y_blocks, axis=1)
        y_off = jnp.concatenate(yoff_blocks, axis=1)
        y = y + y_off * ea_full + dfull_ref[...] * xs
        y = y * _silu(z_ref[r0:r0 + CHUNK, :].astype(F32))
        outs = []
        for g in range(N_GROUPS):
            yg = y[:, g * GROUP_WIDTH:(g + 1) * GROUP_WIDTH]
            ms = jnp.mean(yg * yg, axis=-1, keepdims=True)
            outs.append(yg * lax.rsqrt(ms + EPS))
        yn = jnp.concatenate(outs, axis=1) * g_ref[...]
        y_ref[r0:r0 + CHUNK, :] = yn.astype(y_ref.dtype)

    xpad_ref[0:halo, :] = xpad_ref[SSD_ROWS:SSD_ROWS + halo, :]


def _ssd(xbc, tail, z_ssd, conv_w, conv_b, dtb, a_neg, d_full, g, bsz, seqlen):
    nt = seqlen // SSD_ROWS
    tri3 = _tri3()
    e2 = _head_expand2()
    row = lambda w: pl.BlockSpec((SSD_ROWS, w), lambda b, t: (b * nt + t, 0))
    full = lambda a: pl.BlockSpec(a.shape, lambda b, t: (0, 0))
    return pl.pallas_call(
        _ssd_kernel,
        out_shape=jax.ShapeDtypeStruct((bsz * seqlen, SSD_WIDTH), BF16),
        grid=(bsz, nt),
        in_specs=[row(CONV_CH), row(LANES), row(SSD_WIDTH), full(conv_w), full(conv_b),
                  full(dtb), full(a_neg), full(d_full), full(g), full(tri3), full(e2)],
        out_specs=row(SSD_WIDTH),
        scratch_shapes=[pltpu.VMEM((SSD_ROWS + 2 * SUBLANES, CONV_CH), F32),
                        pltpu.VMEM((D_STATE, SSD_WIDTH), F32)],
        compiler_params=pltpu.CompilerParams(
            dimension_semantics=("arbitrary", "arbitrary"), vmem_limit_bytes=VMEM_LIMIT),
        name="ssd",
    )(xbc, tail, z_ssd, conv_w, conv_b, dtb, a_neg, d_full, g, tri3, e2)


def _fox_attn_kernel(qa_ref, ka_ref, va_ref, z_ref, g_ref, o_ref):
    qi = pl.program_id(2)
    lane = lax.broadcasted_iota(jnp.int32, (ATT_TQ, LANES), 1)
    low = lane < HALF
    row_i = lax.broadcasted_iota(jnp.int32, (ATT_TQ, ATT_TK), 0)
    col_i = lax.broadcasted_iota(jnp.int32, (ATT_TQ, ATT_TK), 1)
    causal = row_i >= col_i
    nt_dims = (((1,), (1,)), ((), ()))

    def head(hh):
        q = qa_ref[0, hh]

        def step(kstart, m, acc, mask):
            k = ka_ref[0, hh, pl.ds(kstart, ATT_TK), :]
            v = va_ref[0, hh, pl.ds(kstart, ATT_TK), :]
            s = lax.dot_general(q, k, nt_dims, preferred_element_type=F32)
            if mask:
                s = jnp.where(causal, s, -jnp.inf)
            m_new = jnp.maximum(m, jnp.max(s, axis=-1, keepdims=True))
            p = jnp.exp2(s - m_new)
            alpha = jnp.exp2(m - m_new)
            acc = alpha * acc + jnp.dot(p.astype(BF16), v, preferred_element_type=F32)
            return m_new, acc

        def body(j, carry):
            m, acc = carry
            return step(pl.multiple_of(j * ATT_TK, ATT_TK), m, acc, False)

        m0 = jnp.full((ATT_TQ, 1), NEG_BIG, F32)
        acc0 = jnp.zeros((ATT_TQ, LANES), F32)
        m, acc = lax.fori_loop(0, qi * (ATT_TQ // ATT_TK), body, (m0, acc0))
        m, acc = step(pl.multiple_of(qi * ATT_TQ, ATT_TQ), m, acc, True)
        l_lane = 0 if hh else HALF
        att = acc / acc[:, l_lane:l_lane + 1]
        mine = jnp.logical_not(low) if hh else low
        ss = jnp.sum(jnp.where(mine, att * att, 0.0), axis=-1, keepdims=True)
        return att * lax.rsqrt(ss * (1.0 / ATT_HEAD_DIM) + EPS)

    even = head(0)
    odd = head(1)
    att = jnp.where(low, even, odd) * g_ref[...]
    o_ref[...] = (att * _silu(z_ref[...].astype(F32))).astype(o_ref.dtype)


def _fox_attn(qa, ka, va, z_att, g2):
    bsz, _, seqlen, _ = qa.shape
    nq = seqlen // ATT_TQ
    q_spec = pl.BlockSpec((1, 2, ATT_TQ, LANES), lambda b, p, i: (b, p, i, 0))
    kv_spec = pl.BlockSpec((1, 2, seqlen, LANES), lambda b, p, i: (b, p, 0, 0))
    row_spec = pl.BlockSpec((ATT_TQ, LANES), lambda b, p, i: (b * nq + i, p))
    return pl.pallas_call(
        _fox_attn_kernel,
        out_shape=jax.ShapeDtypeStruct((bsz * seqlen, ATT_WIDTH), BF16),
        grid=(bsz, ATT_HEADS // 2, nq),
        in_specs=[q_spec, kv_spec, kv_spec, row_spec,
                  pl.BlockSpec(g2.shape, lambda b, p, i: (0, 0))],
        out_specs=row_spec,
        compiler_params=pltpu.CompilerParams(
            dimension_semantics=("arbitrary", "arbitrary", "arbitrary"),
            vmem_limit_bytes=VMEM_LIMIT),
        name="fox_attn",
    )(qa, ka, va, z_att, g2)


def _rms(h, g):
    ms = jnp.mean(h * h, axis=-1, keepdims=True)
    return h * lax.rsqrt(ms + EPS) * g


def _out_ple_kernel(x_ref, ys_ref, ya_ref, p_ref, wo_s_ref, wo_a_ref, wg_ref, wp_ref,
                    gp_ref, gf_ref, o_ref):
    h = x_ref[...]
    h = h + (jnp.dot(ys_ref[...], wo_s_ref[...], preferred_element_type=F32)
             + jnp.dot(ya_ref[...], wo_a_ref[...], preferred_element_type=F32))
    hn = _rms(h, gp_ref[...]).astype(BF16)
    gate = jax.nn.sigmoid(jnp.dot(hn, wg_ref[...], preferred_element_type=F32))
    emb = jnp.dot(p_ref[...].astype(BF16), wp_ref[...], preferred_element_type=F32)
    h = h + gate * emb
    o_ref[...] = _rms(h, gf_ref[...])


def _out_ple(x2, y_ssd, y_att, p2, wo_s, wo_a, wg, wp, gp, gf):
    n = x2.shape[0]
    row = lambda w: pl.BlockSpec((OUT_ROWS, w), lambda i: (i, 0))
    full = lambda a: pl.BlockSpec(a.shape, lambda i: (0, 0))
    return pl.pallas_call(
        _out_ple_kernel,
        out_shape=jax.ShapeDtypeStruct((n, D_MODEL), F32),
        grid=(n // OUT_ROWS,),
        in_specs=[row(D_MODEL), row(SSD_WIDTH), row(ATT_WIDTH), row(PLE_DIM),
                  full(wo_s), full(wo_a), full(wg), full(wp), full(gp), full(gf)],
        out_specs=row(D_MODEL),
        compiler_params=pltpu.CompilerParams(
            dimension_semantics=("arbitrary",), vmem_limit_bytes=VMEM_LIMIT),
        name="out_ple",
    )(x2, y_ssd, y_att, p2, wo_s, wo_a, wg, wp, gp, gf)


def _layer(h2, p2, bsz, seqlen, norm_g, w_in, conv_w, conv_b, dt_bias, a_log, d_skip,
           ssd_norm_g, fg_bias, att_norm_g, w_out, ple_norm_g, w_ple_gate, w_ple_proj, out_g):
    o_zs = 0
    o_xbc = o_zs + SSD_WIDTH
    o_dt = o_xbc + CONV_CH
    o_za = o_dt + SSD_HEADS
    o_q = o_za + ATT_WIDTH
    o_k = o_q + ATT_WIDTH
    o_v = o_k + ATT_WIDTH
    o_f = o_v + ATT_WIDTH
    w_main = jnp.concatenate(
        [w_in[:, o_zs:o_dt], w_in[:, o_za:o_f]], axis=1).astype(BF16)
    w_tail = jnp.concatenate(
        [w_in[:, o_dt:o_za], w_in[:, o_f:o_f + ATT_HEADS],
         jnp.zeros((D_MODEL, LANES - SSD_HEADS - ATT_HEADS), w_in.dtype)], axis=1).astype(BF16)

    def lanes16(v, off):
        return jnp.zeros((1, LANES), F32).at[0, off:off + v.shape[0]].set(v.astype(F32))

    z_ssd, xbc, z_att, q, k, v, tail = _in_proj(
        h2, norm_g.reshape(1, D_MODEL).astype(F32), w_main, w_tail)

    qa, ka, va = _fox_prep(tail, q, k, v, lanes16(fg_bias, TAIL_F), bsz, seqlen)

    y_ssd = _ssd(
        xbc, tail, z_ssd, conv_w.astype(F32), conv_b.reshape(1, CONV_CH).astype(F32),
        lanes16(dt_bias, TAIL_DT), lanes16(-jnp.exp(a_log.astype(F32)), TAIL_DT),
        jnp.repeat(d_skip.astype(F32), SSD_HEAD_DIM).reshape(1, SSD_WIDTH),
        ssd_norm_g.reshape(1, SSD_WIDTH).astype(F32), bsz, seqlen)

    y_att = _fox_attn(qa, ka, va, z_att,
                      jnp.tile(att_norm_g.astype(F32), 2).reshape(1, LANES))

    return _out_ple(
        h2, y_ssd, y_att, p2,
        w_out[:SSD_WIDTH].astype(BF16), w_out[SSD_WIDTH:].astype(BF16),
        w_ple_gate.astype(BF16), w_ple_proj.astype(BF16),
        ple_norm_g.reshape(1, D_MODEL).astype(F32), out_g.reshape(1, D_MODEL).astype(F32))


def kernel(x, p, norm_g, w_in, conv_w, conv_b, dt_bias, a_log, d_skip, ssd_norm_g, fg_bias,
           att_norm_g, w_out, ple_norm_g, w_ple_gate, w_ple_proj, final_norm_g):
    bsz, seqlen, _ = x.shape
    depth = p.shape[0]
    assert depth == 1, "the fused tail applies the final norm right after the only layer"
    h2 = x.reshape(bsz * seqlen, D_MODEL)
    out = _layer(h2, p[0].reshape(bsz * seqlen, PLE_DIM), bsz, seqlen,
                 norm_g[0], w_in[0], conv_w[0], conv_b[0], dt_bias[0], a_log[0], d_skip[0],
                 ssd_norm_g[0], fg_bias[0], att_norm_g[0], w_out[0], ple_norm_g[0],
                 w_ple_gate[0], w_ple_proj[0], final_norm_g)
    return out.reshape(bsz, seqlen, D_MODEL)
```

```python
import functools
import math

import jax
import jax.numpy as jnp
import numpy as np
from jax import lax
from jax.experimental import pallas as pl
from jax.experimental.pallas import tpu as pltpu

F32 = jnp.float32
BF16 = jnp.bfloat16

D_MODEL = 1024
SSD_WIDTH = 1024
ATT_WIDTH = 1024
SSD_HEAD_DIM = 64
SSD_HEADS = 16
N_GROUPS = 2
HEADS_PER_GROUP = SSD_HEADS // N_GROUPS
GROUP_WIDTH = SSD_WIDTH // N_GROUPS
D_STATE = 128
CONV_WIDTH = 4
CHUNK = 128
ATT_HEAD_DIM = 64
ATT_HEADS = 16
PLE_DIM = 256
EPS = 1e-6
CONV_CH = SSD_WIDTH + 2 * N_GROUPS * D_STATE

LANES = 128
SUBLANES = 8
HALF = LANES // 2

LOG2E = 1.4426950408889634
NEG_BIG = -1e30

SEC_ZS = (0, SSD_WIDTH)
SEC_XBC = (SEC_ZS[0] + SEC_ZS[1], CONV_CH)
SEC_ZA = (SEC_XBC[0] + SEC_XBC[1], ATT_WIDTH)
SEC_Q = (SEC_ZA[0] + SEC_ZA[1], ATT_WIDTH)
SEC_K = (SEC_Q[0] + SEC_Q[1], ATT_WIDTH)
SEC_V = (SEC_K[0] + SEC_K[1], ATT_WIDTH)
MAIN_COLS = SEC_V[0] + SEC_V[1]
TAIL_DT = 0
TAIL_F = SSD_HEADS

IN_ROWS = 512
IN_COL_CHUNK = 512
PREP_ROWS = 512
SSD_ROWS = 256
ATT_TQ = 1024
ATT_TK = 512
OUT_ROWS = 512
VMEM_LIMIT = 56 * 1024 * 1024


def _split_bf16(x, parts):
    out = []
    r = x
    for i in range(parts):
        h = r.astype(BF16)
        out.append(h)
        if i + 1 < parts:
            r = r - h.astype(F32)
    return out


def _softplus(x):
    return jnp.maximum(x, 0.0) + jnp.log1p(jnp.exp(-jnp.abs(x)))


def _silu(x):
    return x * jax.nn.sigmoid(x)


def _in_proj_kernel(x_ref, g_ref, wm_ref, wt_ref,
                    zs_ref, xbc_ref, za_ref, q_ref, k_ref, v_ref, tail_ref):
    x = x_ref[...]
    ms = jnp.mean(x * x, axis=-1, keepdims=True)
    u = (x * lax.rsqrt(ms + EPS) * g_ref[...]).astype(BF16)

    def section(out_ref, sec, mul=None):
        off, width = sec
        for c in range(0, width, IN_COL_CHUNK):
            acc = jnp.dot(u, wm_ref[:, off + c:off + c + IN_COL_CHUNK],
                          preferred_element_type=F32)
            if mul is not None:
                acc = acc * mul
            out_ref[:, c:c + IN_COL_CHUNK] = acc.astype(out_ref.dtype)

    section(zs_ref, SEC_ZS)
    section(xbc_ref, SEC_XBC)
    section(za_ref, SEC_ZA)
    section(q_ref, SEC_Q, mul=ATT_HEAD_DIM ** -0.5 * LOG2E)
    section(k_ref, SEC_K)
    section(v_ref, SEC_V)
    tail_ref[...] = jnp.dot(u, wt_ref[...], preferred_element_type=F32)


def _in_proj(x2, g, w_main, w_tail):
    n = x2.shape[0]
    row = lambda w: pl.BlockSpec((IN_ROWS, w), lambda i: (i, 0))
    full = lambda a: pl.BlockSpec(a.shape, lambda i: (0, 0))
    out_shapes = (
        jax.ShapeDtypeStruct((n, SSD_WIDTH), BF16),
        jax.ShapeDtypeStruct((n, CONV_CH), BF16),
        jax.ShapeDtypeStruct((n, ATT_WIDTH), BF16),
        jax.ShapeDtypeStruct((n, ATT_WIDTH), BF16),
        jax.ShapeDtypeStruct((n, ATT_WIDTH), BF16),
        jax.ShapeDtypeStruct((n, ATT_WIDTH), BF16),
        jax.ShapeDtypeStruct((n, LANES), F32),
    )
    return pl.pallas_call(
        _in_proj_kernel,
        out_shape=out_shapes,
        grid=(n // IN_ROWS,),
        in_specs=[row(D_MODEL), full(g), full(w_main), full(w_tail)],
        out_specs=(row(SSD_WIDTH), row(CONV_CH), row(ATT_WIDTH), row(ATT_WIDTH),
                   row(ATT_WIDTH), row(ATT_WIDTH), row(LANES)),
        compiler_params=pltpu.CompilerParams(
            dimension_semantics=("arbitrary",), vmem_limit_bytes=VMEM_LIMIT),
        name="in_proj",
    )(x2, g, w_main, w_tail)


def _aug_lane(head, j):
    pair, odd = divmod(head, 2)
    return pair * LANES + (j if odd else HALF + j)


def _aug_matrices():
    eq = np.zeros((4 * LANES, ATT_WIDTH), np.float32)
    ek = np.zeros((4 * LANES, ATT_WIDTH), np.float32)
    for h in range(ATT_HEADS):
        src = TAIL_F + h
        for part in range(3):
            eq[part * LANES + src, _aug_lane(h, part)] = 1.0
            ek[part * LANES + src, _aug_lane(h, 3 + part)] = -1.0
            eq[3 * LANES, _aug_lane(h, 3 + part)] = 1.0
            ek[3 * LANES, _aug_lane(h, part)] = 1.0
    return jnp.asarray(eq, BF16), jnp.asarray(ek, BF16)


def _tri3():
    tri = np.tril(np.ones((CHUNK, CHUNK), np.float32))
    return jnp.asarray(np.concatenate([tri, tri, tri], axis=1), BF16)


def _cumsum_rows(tri3, v):
    parts = _split_bf16(v, 3)
    return jnp.dot(tri3, jnp.concatenate(parts, axis=0), preferred_element_type=F32)


def _fox_prep_kernel(tail_ref, q_ref, k_ref, v_ref, fgb_ref, tri3_ref, eq_ref, ek_ref,
                     qa_ref, ka_ref, va_ref, carry_ref):
    @pl.when(pl.program_id(1) == 0)
    def _():
        carry_ref[...] = jnp.zeros_like(carry_ref)

    tri3 = tri3_ref[...]
    lane = lax.broadcasted_iota(jnp.int32, (CHUNK, LANES), 1)
    low = lane < HALF
    one_even = jnp.where(lane == HALF, 1.0, 0.0).astype(BF16)
    one_odd = jnp.where(lane == 0, 1.0, 0.0).astype(BF16)
    ones = jnp.ones((CHUNK, LANES), BF16)
    carry = carry_ref[...]
    for c in range(PREP_ROWS // CHUNK):
        rows = pl.ds(c * CHUNK, CHUNK)
        xf = tail_ref[rows, :] + fgb_ref[...]
        log_f = -_softplus(-xf)
        cum = _cumsum_rows(tri3, log_f) + carry
        carry = cum[CHUNK - 1:CHUNK, :]
        parts = _split_bf16(cum * LOG2E, 3)
        lhs = jnp.concatenate(parts + [ones], axis=1)
        aq = jnp.dot(lhs, eq_ref[...], preferred_element_type=F32).astype(BF16)
        ak = jnp.dot(lhs, ek_ref[...], preferred_element_type=F32).astype(BF16)
        for pair in range(ATT_HEADS // 2):
            cols = slice(pair * LANES, (pair + 1) * LANES)
            q2 = q_ref[rows, cols]
            k2 = k_ref[rows, cols]
            v2 = v_ref[rows, cols]
            qa_ref[0, 2 * pair, rows, :] = jnp.where(low, q2, aq[:, cols])
            qa_ref[0, 2 * pair + 1, rows, :] = jnp.where(low, aq[:, cols], q2)
            ka_ref[0, 2 * pair, rows, :] = jnp.where(low, k2, ak[:, cols])
            ka_ref[0, 2 * pair + 1, rows, :] = jnp.where(low, ak[:, cols], k2)
            va_ref[0, 2 * pair, rows, :] = jnp.where(low, v2, one_even)
            va_ref[0, 2 * pair + 1, rows, :] = jnp.where(low, one_odd, v2)
    carry_ref[...] = carry


def _fox_prep(tail, q, k, v, fgb, bsz, seqlen):
    nt = seqlen // PREP_ROWS
    tri3 = _tri3()
    eq, ek = _aug_matrices()
    row = lambda w: pl.BlockSpec((PREP_ROWS, w), lambda b, t: (b * nt + t, 0))
    full = lambda a: pl.BlockSpec(a.shape, lambda b, t: (0, 0))
    head_major = pl.BlockSpec((1, ATT_HEADS, PREP_ROWS, LANES), lambda b, t: (b, 0, t, 0))
    aug_shape = jax.ShapeDtypeStruct((bsz, ATT_HEADS, seqlen, LANES), BF16)
    return pl.pallas_call(
        _fox_prep_kernel,
        out_shape=(aug_shape, aug_shape, aug_shape),
        grid=(bsz, nt),
        in_specs=[row(LANES), row(ATT_WIDTH), row(ATT_WIDTH), row(ATT_WIDTH),
                  full(fgb), full(tri3), full(eq), full(ek)],
        out_specs=(head_major, head_major, head_major),
        scratch_shapes=[pltpu.VMEM((1, LANES), F32)],
        compiler_params=pltpu.CompilerParams(
            dimension_semantics=("arbitrary", "arbitrary"), vmem_limit_bytes=VMEM_LIMIT),
        name="fox_prep",
    )(tail, q, k, v, fgb, tri3, eq, ek)


def _head_expand2():
    e = np.zeros((2 * LANES, SSD_WIDTH), np.float32)
    for h in range(SSD_HEADS):
        e[TAIL_DT + h, h * SSD_HEAD_DIM:(h + 1) * SSD_HEAD_DIM] = 1.0
        e[LANES + TAIL_DT + h, h * SSD_HEAD_DIM:(h + 1) * SSD_HEAD_DIM] = 1.0
    return jnp.asarray(e, BF16)


def _ssd_kernel(xbc_ref, tail_ref, z_ref, cw_ref, cb_ref, dtb_ref, a_ref, dfull_ref, g_ref,
                tri3_ref, e2_ref, y_ref, xpad_ref, state_ref):
    halo = SUBLANES

    @pl.when(pl.program_id(1) == 0)
    def _():
        xpad_ref[0:halo, :] = jnp.zeros((halo, CONV_CH), F32)
        state_ref[...] = jnp.zeros_like(state_ref)

    xpad_ref[halo:halo + SSD_ROWS, :] = xbc_ref[...].astype(F32)

    tri3 = tri3_ref[...]
    row_i = lax.broadcasted_iota(jnp.int32, (CHUNK, CHUNK), 0)
    col_i = lax.broadcasted_iota(jnp.int32, (CHUNK, CHUNK), 1)
    causal = row_i >= col_i
    low = col_i < HALF

    for c in range(SSD_ROWS // CHUNK):
        r0 = c * CHUNK
        conv = cb_ref[...]
        for j in range(CONV_WIDTH):
            start = halo + r0 - (CONV_WIDTH - 1) + j
            conv = conv + cw_ref[j:j + 1, :] * xpad_ref[start:start + CHUNK, :]
        xc = _silu(conv)
        xs = xc[:, :SSD_WIDTH]
        bm = xc[:, SSD_WIDTH:SSD_WIDTH + N_GROUPS * D_STATE].astype(BF16)
        cm = xc[:, SSD_WIDTH + N_GROUPS * D_STATE:].astype(BF16)

        dt = _softplus(tail_ref[r0:r0 + CHUNK, :] + dtb_ref[...])
        adt = dt * a_ref[...]
        acs = _cumsum_rows(tri3, adt)
        acs_t = acs.T
        last = acs[CHUNK - 1:CHUNK, :]
        w = dt * jnp.exp(last - acs)
        ea = jnp.exp(acs)
        cd = jnp.broadcast_to(jnp.exp(last), (SUBLANES, LANES))
        stack = jnp.concatenate([dt, w, ea, cd], axis=0)
        lhs = jnp.concatenate(_split_bf16(stack, 2), axis=1)
        spread = jnp.dot(lhs, e2_ref[...], preferred_element_type=F32)
        dt_full = spread[0:CHUNK]
        w_full = spread[CHUNK:2 * CHUNK]
        ea_full = spread[2 * CHUNK:3 * CHUNK]
        cd_full = spread[3 * CHUNK:3 * CHUNK + 1]

        xdt = (xs * dt_full).astype(BF16)
        wx = (xs * w_full).astype(BF16)

        y_blocks = []
        yoff_blocks = []
        for g in range(N_GROUPS):
            gs = slice(g * D_STATE, (g + 1) * D_STATE)
            cb_g = lax.dot_general(cm[:, gs], bm[:, gs], (((1,), (1,)), ((), ())),
                                   preferred_element_type=F32)
            for jp in range(HEADS_PER_GROUP // 2):
                pair = g * (HEADS_PER_GROUP // 2) + jp
                x_pair = xdt[:, pair * LANES:(pair + 1) * LANES]
                ys = []
                for r in (2 * pair, 2 * pair + 1):
                    seg = acs[:, r:r + 1] - acs_t[r:r + 1, :]
                    dec = jnp.exp(jnp.where(causal, seg, -jnp.inf))
                    m = (cb_g * dec).astype(BF16)
                    ys.append(jnp.dot(m, x_pair, preferred_element_type=F32))
                y_blocks.append(jnp.where(low, ys[0], ys[1]))
            hs = slice(g * GROUP_WIDTH, (g + 1) * GROUP_WIDTH)
            st = state_ref[:, hs]
            yoff_blocks.append(jnp.dot(cm[:, gs], st.astype(BF16), preferred_element_type=F32))
            s_g = lax.dot_general(bm[:, gs], wx[:, hs], (((0,), (0,)), ((), ())),
                                  preferred_element_type=F32)
            state_ref[:, hs] = st * cd_full[:, hs] + s_g
        y = jnp.concatenate(y_blocks, axis=1)
        y_off = jnp.concatenate(yoff_blocks, axis=1)
        y = y + y_off * ea_full + dfull_ref[...] * xs
        y = y * _silu(z_ref[r0:r0 + CHUNK, :].astype(F32))
        outs = []
        for g in range(N_GROUPS):
            yg = y[:, g * GROUP_WIDTH:(g + 1) * GROUP_WIDTH]
            ms = jnp.mean(yg * yg, axis=-1, keepdims=True)
            outs.append(yg * lax.rsqrt(ms + EPS))
        yn = jnp.concatenate(outs, axis=1) * g_ref[...]
        y_ref[r0:r0 + CHUNK, :] = yn.astype(y_ref.dtype)

    xpad_ref[0:halo, :] = xpad_ref[SSD_ROWS:SSD_ROWS + halo, :]


def _ssd(xbc, tail, z_ssd, conv_w, conv_b, dtb, a_neg, d_full, g, bsz, seqlen):
    nt = seqlen // SSD_ROWS
    tri3 = _tri3()
    e2 = _head_expand2()
    row = lambda w: pl.BlockSpec((SSD_ROWS, w), lambda b, t: (b * nt + t, 0))
    full = lambda a: pl.BlockSpec(a.shape, lambda b, t: (0, 0))
    return pl.pallas_call(
        _ssd_kernel,
        out_shape=jax.ShapeDtypeStruct((bsz * seqlen, SSD_WIDTH), BF16),
        grid=(bsz, nt),
        in_specs=[row(CONV_CH), row(LANES), row(SSD_WIDTH), full(conv_w), full(conv_b),
                  full(dtb), full(a_neg), full(d_full), full(g), full(tri3), full(e2)],
        out_specs=row(SSD_WIDTH),
        scratch_shapes=[pltpu.VMEM((SSD_ROWS + 2 * SUBLANES, CONV_CH), F32),
                        pltpu.VMEM((D_STATE, SSD_WIDTH), F32)],
        compiler_params=pltpu.CompilerParams(
            dimension_semantics=("arbitrary", "arbitrary"), vmem_limit_bytes=VMEM_LIMIT),
        name="ssd",
    )(xbc, tail, z_ssd, conv_w, conv_b, dtb, a_neg, d_full, g, tri3, e2)


def _fox_attn_kernel(qa_ref, ka_ref, va_ref, z_ref, g_ref, o_ref, s_ref, acc_ref, m_ref):
    qi = pl.program_id(2)
    lane = lax.broadcasted_iota(jnp.int32, (ATT_TQ, LANES), 1)
    low = lane < HALF
    nt_dims = (((1,), (1,)), ((), ()))
    n_diag = ATT_TQ // ATT_TK
    all_rows = (0, ATT_TQ)

    def scores(hh, kstart, rows=all_rows):
        k = ka_ref[0, hh, pl.ds(kstart, ATT_TK), :]
        s_ref[hh, rows[0]:rows[1], :] = lax.dot_general(
            qa_ref[0, hh, rows[0]:rows[1], :], k, nt_dims, preferred_element_type=F32)

    def softmax_pv(hh, kstart, rows=all_rows, diag=None):
        r0, r1 = rows
        s = s_ref[hh, r0:r1, :]
        if diag is not None:
            row_i = lax.broadcasted_iota(jnp.int32, (r1 - r0, ATT_TK), 0) + r0
            col_i = lax.broadcasted_iota(jnp.int32, (r1 - r0, ATT_TK), 1) + diag * ATT_TK
            s = jnp.where(row_i >= col_i, s, -jnp.inf)
        m_old = m_ref[hh, r0:r1, :]
        m_new = jnp.maximum(m_old, jnp.max(s, axis=-1, keepdims=True))
        p = jnp.exp2(s - jnp.tile(m_new, (1, ATT_TK // LANES))).astype(BF16)
        alpha = jnp.exp2(m_old - m_new)
        m_ref[hh, r0:r1, :] = m_new
        v = va_ref[0, hh, pl.ds(kstart, ATT_TK), :]
        acc_ref[hh, r0:r1, :] = (alpha * acc_ref[hh, r0:r1, :]
                                 + jnp.dot(p, v, preferred_element_type=F32))

    m_ref[...] = jnp.full(m_ref.shape, NEG_BIG, F32)
    acc_ref[...] = jnp.zeros(acc_ref.shape, F32)
    scores(0, 0)

    def body(j, carry):
        kstart = pl.multiple_of(j * ATT_TK, ATT_TK)
        scores(1, kstart)
        softmax_pv(0, kstart)
        scores(0, kstart + ATT_TK)
        softmax_pv(1, kstart)
        return carry

    lax.fori_loop(0, qi * n_diag, body, 0)
    kdiag = pl.multiple_of(qi * ATT_TQ, ATT_TQ)
    for d in range(n_diag):
        rows = (d * ATT_TK, ATT_TQ)
        scores(1, kdiag + d * ATT_TK, rows)
        softmax_pv(0, kdiag + d * ATT_TK, rows, diag=d)
        if d + 1 < n_diag:
            scores(0, kdiag + (d + 1) * ATT_TK, ((d + 1) * ATT_TK, ATT_TQ))
        softmax_pv(1, kdiag + d * ATT_TK, rows, diag=d)

    def normed(hh):
        acc = acc_ref[hh]
        l_lane = 0 if hh else HALF
        mine = jnp.logical_not(low) if hh else low
        l = jnp.sum(jnp.where(lane == l_lane, acc, 0.0), axis=-1, keepdims=True)
        ssq = jnp.sum(jnp.where(mine, acc * acc, 0.0), axis=-1, keepdims=True)
        inv_l = 1.0 / l
        return acc * (inv_l * lax.rsqrt(ssq * (inv_l * inv_l) * (1.0 / ATT_HEAD_DIM) + EPS))

    att = jnp.where(low, normed(0), normed(1)) * g_ref[...]
    o_ref[...] = (att * _silu(z_ref[...].astype(F32))).astype(o_ref.dtype)


def _fox_attn(qa, ka, va, z_att, g2):
    assert ATT_TQ % ATT_TK == 0
    bsz, _, seqlen, _ = qa.shape
    nq = seqlen // ATT_TQ
    q_spec = pl.BlockSpec((1, 2, ATT_TQ, LANES), lambda b, p, i: (b, p, i, 0))
    kv_spec = pl.BlockSpec((1, 2, seqlen, LANES), lambda b, p, i: (b, p, 0, 0))
    row_spec = pl.BlockSpec((ATT_TQ, LANES), lambda b, p, i: (b * nq + i, p))
    return pl.pallas_call(
        _fox_attn_kernel,
        out_shape=jax.ShapeDtypeStruct((bsz * seqlen, ATT_WIDTH), BF16),
        grid=(bsz, ATT_HEADS // 2, nq),
        in_specs=[q_spec, kv_spec, kv_spec, row_spec,
                  pl.BlockSpec(g2.shape, lambda b, p, i: (0, 0))],
        out_specs=row_spec,
        scratch_shapes=[pltpu.VMEM((2, ATT_TQ, ATT_TK), F32),
                        pltpu.VMEM((2, ATT_TQ, LANES), F32),
                        pltpu.VMEM((2, ATT_TQ, LANES), F32)],
        compiler_params=pltpu.CompilerParams(
            dimension_semantics=("arbitrary", "arbitrary", "arbitrary"),
            vmem_limit_bytes=VMEM_LIMIT),
        name="fox_attn",
    )(qa, ka, va, z_att, g2)


def _rms(h, g):
    ms = jnp.mean(h * h, axis=-1, keepdims=True)
    return h * lax.rsqrt(ms + EPS) * g


def _out_ple_kernel(x_ref, ys_ref, ya_ref, p_ref, wo_s_ref, wo_a_ref, wg_ref, wp_ref,
                    gp_ref, gf_ref, o_ref):
    h = x_ref[...]
    h = h + (jnp.dot(ys_ref[...], wo_s_ref[...], preferred_element_type=F32)
             + jnp.dot(ya_ref[...], wo_a_ref[...], preferred_element_type=F32))
    hn = _rms(h, gp_ref[...]).astype(BF16)
    gate = jax.nn.sigmoid(jnp.dot(hn, wg_ref[...], preferred_element_type=F32))
    emb = jnp.dot(p_ref[...].astype(BF16), wp_ref[...], preferred_element_type=F32)
    h = h + gate * emb
    o_ref[...] = _rms(h, gf_ref[...])


def _out_ple(x2, y_ssd, y_att, p2, wo_s, wo_a, wg, wp, gp, gf):
    n = x2.shape[0]
    row = lambda w: pl.BlockSpec((OUT_ROWS, w), lambda i: (i, 0))
    full = lambda a: pl.BlockSpec(a.shape, lambda i: (0, 0))
    return pl.pallas_call(
        _out_ple_kernel,
        out_shape=jax.ShapeDtypeStruct((n, D_MODEL), F32),
        grid=(n // OUT_ROWS,),
        in_specs=[row(D_MODEL), row(SSD_WIDTH), row(ATT_WIDTH), row(PLE_DIM),
                  full(wo_s), full(wo_a), full(wg), full(wp), full(gp), full(gf)],
        out_specs=row(D_MODEL),
        compiler_params=pltpu.CompilerParams(
            dimension_semantics=("arbitrary",), vmem_limit_bytes=VMEM_LIMIT),
        name="out_ple",
    )(x2, y_ssd, y_att, p2, wo_s, wo_a, wg, wp, gp, gf)


def _layer(h2, p2, bsz, seqlen, norm_g, w_in, conv_w, conv_b, dt_bias, a_log, d_skip,
           ssd_norm_g, fg_bias, att_norm_g, w_out, ple_norm_g, w_ple_gate, w_ple_proj, out_g):
    o_zs = 0
    o_xbc = o_zs + SSD_WIDTH
    o_dt = o_xbc + CONV_CH
    o_za = o_dt + SSD_HEADS
    o_q = o_za + ATT_WIDTH
    o_k = o_q + ATT_WIDTH
    o_v = o_k + ATT_WIDTH
    o_f = o_v + ATT_WIDTH
    w_main = jnp.concatenate(
        [w_in[:, o_zs:o_dt], w_in[:, o_za:o_f]], axis=1).astype(BF16)
    w_tail = jnp.concatenate(
        [w_in[:, o_dt:o_za], w_in[:, o_f:o_f + ATT_HEADS],
         jnp.zeros((D_MODEL, LANES - SSD_HEADS - ATT_HEADS), w_in.dtype)], axis=1).astype(BF16)

    def lanes16(v, off):
        return jnp.zeros((1, LANES), F32).at[0, off:off + v.shape[0]].set(v.astype(F32))

    z_ssd, xbc, z_att, q, k, v, tail = _in_proj(
        h2, norm_g.reshape(1, D_MODEL).astype(F32), w_main, w_tail)

    qa, ka, va = _fox_prep(tail, q, k, v, lanes16(fg_bias, TAIL_F), bsz, seqlen)

    y_ssd = _ssd(
        xbc, tail, z_ssd, conv_w.astype(F32), conv_b.reshape(1, CONV_CH).astype(F32),
        lanes16(dt_bias, TAIL_DT), lanes16(-jnp.exp(a_log.astype(F32)), TAIL_DT),
        jnp.repeat(d_skip.astype(F32), SSD_HEAD_DIM).reshape(1, SSD_WIDTH),
        ssd_norm_g.reshape(1, SSD_WIDTH).astype(F32), bsz, seqlen)

    y_att = _fox_attn(qa, ka, va, z_att,
                      jnp.tile(att_norm_g.astype(F32), 2).reshape(1, LANES))

    return _out_ple(
        h2, y_ssd, y_att, p2,
        w_out[:SSD_WIDTH].astype(BF16), w_out[SSD_WIDTH:].astype(BF16),
        w_ple_gate.astype(BF16), w_ple_proj.astype(BF16),
        ple_norm_g.reshape(1, D_MODEL).astype(F32), out_g.reshape(1, D_MODEL).astype(F32))


def kernel(x, p, norm_g, w_in, conv_w, conv_b, dt_bias, a_log, d_skip, ssd_norm_g, fg_bias,
           att_norm_g, w_out, ple_norm_g, w_ple_gate, w_ple_proj, final_norm_g):
    bsz, seqlen, _ = x.shape
    depth = p.shape[0]
    assert depth == 1, "the fused tail applies the final norm right after the only layer"
    h2 = x.reshape(bsz * seqlen, D_MODEL)
    out = _layer(h2, p[0].reshape(bsz * seqlen, PLE_DIM), bsz, seqlen,
                 norm_g[0], w_in[0], conv_w[0], conv_b[0], dt_bias[0], a_log[0], d_skip[0],
                 ssd_norm_g[0], fg_bias[0], att_norm_g[0], w_out[0], ple_norm_g[0],
                 w_ple_gate[0], w_ple_proj[0], final_norm_g)
    return out.reshape(bsz, seqlen, D_MODEL)
```

```python
import functools
import math

import jax
import jax.numpy as jnp
import numpy as np
from jax import lax
from jax.experimental import pallas as pl
from jax.experimental.pallas import tpu as pltpu

F32 = jnp.float32
BF16 = jnp.bfloat16

D_MODEL = 1024
SSD_WIDTH = 1024
ATT_WIDTH = 1024
SSD_HEAD_DIM = 64
SSD_HEADS = 16
N_GROUPS = 2
HEADS_PER_GROUP = SSD_HEADS // N_GROUPS
GROUP_WIDTH = SSD_WIDTH // N_GROUPS
D_STATE = 128
CONV_WIDTH = 4
CHUNK = 128
ATT_HEAD_DIM = 64
ATT_HEADS = 16
PLE_DIM = 256
EPS = 1e-6
CONV_CH = SSD_WIDTH + 2 * N_GROUPS * D_STATE

LANES = 128
SUBLANES = 8
HALF = LANES // 2

LOG2E = 1.4426950408889634
NEG_BIG = -1e30

SEC_ZS = (0, SSD_WIDTH)
SEC_XBC = (SEC_ZS[0] + SEC_ZS[1], CONV_CH)
SEC_ZA = (SEC_XBC[0] + SEC_XBC[1], ATT_WIDTH)
SEC_Q = (SEC_ZA[0] + SEC_ZA[1], ATT_WIDTH)
SEC_K = (SEC_Q[0] + SEC_Q[1], ATT_WIDTH)
SEC_V = (SEC_K[0] + SEC_K[1], ATT_WIDTH)
MAIN_COLS = SEC_V[0] + SEC_V[1]
TAIL_DT = 0
TAIL_F = SSD_HEADS

IN_ROWS = 512
IN_COL_CHUNK = 512
SSD_ROWS = 256
ATT_TQ = 1024
ATT_TK = 512
OUT_ROWS = 512
VMEM_LIMIT = 56 * 1024 * 1024


def _split_bf16(x, parts):
    out = []
    r = x
    for i in range(parts):
        h = r.astype(BF16)
        out.append(h)
        if i + 1 < parts:
            r = r - h.astype(F32)
    return out


def _softplus(x):
    return jnp.maximum(x, 0.0) + jnp.log1p(jnp.exp(-jnp.abs(x)))


def _silu(x):
    return x * jax.nn.sigmoid(x)


def _aug_lane(head, j):
    pair, odd = divmod(head, 2)
    return pair * LANES + (j if odd else HALF + j)


def _aug_matrices():
    eq = np.zeros((4 * LANES, ATT_WIDTH), np.float32)
    ek = np.zeros((4 * LANES, ATT_WIDTH), np.float32)
    for h in range(ATT_HEADS):
        src = TAIL_F + h
        for part in range(3):
            eq[part * LANES + src, _aug_lane(h, part)] = 1.0
            ek[part * LANES + src, _aug_lane(h, 3 + part)] = -1.0
            eq[3 * LANES, _aug_lane(h, 3 + part)] = 1.0
            ek[3 * LANES, _aug_lane(h, part)] = 1.0
    return jnp.asarray(eq, BF16), jnp.asarray(ek, BF16)


def _tri3():
    tri = np.tril(np.ones((CHUNK, CHUNK), np.float32))
    return jnp.asarray(np.concatenate([tri, tri, tri], axis=1), BF16)


def _cumsum_rows(tri3, v):
    parts = _split_bf16(v, 3)
    return jnp.dot(tri3, jnp.concatenate(parts, axis=0), preferred_element_type=F32)


def _in_proj_kernel(x_ref, g_ref, wm_ref, wt_ref, fgb_ref, tri3_ref, eq_ref, ek_ref,
                    zs_ref, xbc_ref, za_ref, qa_ref, ka_ref, va_ref, tail_ref, carry_ref):
    @pl.when(pl.program_id(1) == 0)
    def _():
        carry_ref[...] = jnp.zeros_like(carry_ref)

    x = x_ref[...]
    ms = jnp.mean(x * x, axis=-1, keepdims=True)
    u = (x * lax.rsqrt(ms + EPS) * g_ref[...]).astype(BF16)

    def project(sec, c):
        return jnp.dot(u, wm_ref[:, sec[0] + c:sec[0] + c + IN_COL_CHUNK],
                       preferred_element_type=F32)

    def section(out_ref, sec):
        for c in range(0, sec[1], IN_COL_CHUNK):
            out_ref[:, c:c + IN_COL_CHUNK] = project(sec, c).astype(out_ref.dtype)

    tail = jnp.dot(u, wt_ref[...], preferred_element_type=F32)
    tail_ref[...] = tail

    tri3 = tri3_ref[...]
    ones = jnp.ones((CHUNK, LANES), BF16)
    carry = carry_ref[...]
    lhs = []
    for c in range(IN_ROWS // CHUNK):
        log_f = -_softplus(-(tail[c * CHUNK:(c + 1) * CHUNK] + fgb_ref[...]))
        cum = _cumsum_rows(tri3, log_f) + carry
        carry = cum[CHUNK - 1:CHUNK, :]
        lhs.append(jnp.concatenate(_split_bf16(cum * LOG2E, 3) + [ones], axis=1))
    carry_ref[...] = carry
    lhs = jnp.concatenate(lhs, axis=0)
    aq = jnp.dot(lhs, eq_ref[...], preferred_element_type=F32).astype(BF16)
    ak = jnp.dot(lhs, ek_ref[...], preferred_element_type=F32).astype(BF16)

    section(zs_ref, SEC_ZS)
    section(xbc_ref, SEC_XBC)
    section(za_ref, SEC_ZA)

    lane = lax.broadcasted_iota(jnp.int32, (IN_ROWS, LANES), 1)
    low = lane < HALF
    one_even = jnp.where(lane == HALF, 1.0, 0.0).astype(BF16)
    one_odd = jnp.where(lane == 0, 1.0, 0.0).astype(BF16)

    def head_major(out_ref, sec, aug_even, aug_odd, mul=None):
        for c in range(0, sec[1], IN_COL_CHUNK):
            acc = project(sec, c)
            if mul is not None:
                acc = acc * mul
            acc = acc.astype(BF16)
            for i in range(IN_COL_CHUNK // LANES):
                pair = c // LANES + i
                d2 = acc[:, i * LANES:(i + 1) * LANES]
                out_ref[0, 2 * pair] = jnp.where(low, d2, aug_even(pair))
                out_ref[0, 2 * pair + 1] = jnp.where(low, aug_odd(pair), d2)

    pair_cols = lambda a: (lambda pair: a[:, pair * LANES:(pair + 1) * LANES])
    head_major(qa_ref, SEC_Q, pair_cols(aq), pair_cols(aq), mul=ATT_HEAD_DIM ** -0.5 * LOG2E)
    head_major(ka_ref, SEC_K, pair_cols(ak), pair_cols(ak))
    head_major(va_ref, SEC_V, lambda pair: one_even, lambda pair: one_odd)


def _const_spec(a):
    zeros = (0,) * a.ndim
    return pl.BlockSpec(a.shape, lambda *_: zeros, pipeline_mode=pl.Buffered(1))


def _in_proj(x2, g, w_main, w_tail, fgb, bsz, seqlen):
    n = x2.shape[0]
    nt = seqlen // IN_ROWS
    tri3 = _tri3()
    eq, ek = _aug_matrices()
    row = lambda w: pl.BlockSpec((IN_ROWS, w), lambda b, t: (b * nt + t, 0))
    full = _const_spec
    head_major = pl.BlockSpec((1, ATT_HEADS, IN_ROWS, LANES), lambda b, t: (b, 0, t, 0))
    aug_shape = jax.ShapeDtypeStruct((bsz, ATT_HEADS, seqlen, LANES), BF16)
    out_shapes = (
        jax.ShapeDtypeStruct((n, SSD_WIDTH), BF16),
        jax.ShapeDtypeStruct((n, CONV_CH), BF16),
        jax.ShapeDtypeStruct((n, ATT_WIDTH), BF16),
        aug_shape, aug_shape, aug_shape,
        jax.ShapeDtypeStruct((n, LANES), F32),
    )
    return pl.pallas_call(
        _in_proj_kernel,
        out_shape=out_shapes,
        grid=(bsz, nt),
        in_specs=[row(D_MODEL), full(g), full(w_main), full(w_tail), full(fgb),
                  full(tri3), full(eq), full(ek)],
        out_specs=(row(SSD_WIDTH), row(CONV_CH), row(ATT_WIDTH),
                   head_major, head_major, head_major, row(LANES)),
        scratch_shapes=[pltpu.VMEM((1, LANES), F32)],
        compiler_params=pltpu.CompilerParams(
            dimension_semantics=("arbitrary", "arbitrary"), vmem_limit_bytes=VMEM_LIMIT),
        name="in_proj",
    )(x2, g, w_main, w_tail, fgb, tri3, eq, ek)


def _head_expand2():
    e = np.zeros((2 * LANES, SSD_WIDTH), np.float32)
    for h in range(SSD_HEADS):
        e[TAIL_DT + h, h * SSD_HEAD_DIM:(h + 1) * SSD_HEAD_DIM] = 1.0
        e[LANES + TAIL_DT + h, h * SSD_HEAD_DIM:(h + 1) * SSD_HEAD_DIM] = 1.0
    return jnp.asarray(e, BF16)


def _ssd_kernel(xbc_ref, tail_ref, z_ref, cw_ref, cb_ref, dtb_ref, a_ref, dfull_ref, g_ref,
                tri3_ref, e2_ref, y_ref, xpad_ref, state_ref):
    halo = SUBLANES

    @pl.when(pl.program_id(1) == 0)
    def _():
        xpad_ref[0:halo, :] = jnp.zeros((halo, CONV_CH), F32)
        state_ref[...] = jnp.zeros_like(state_ref)

    xpad_ref[halo:halo + SSD_ROWS, :] = xbc_ref[...].astype(F32)

    tri3 = tri3_ref[...]
    row_i = lax.broadcasted_iota(jnp.int32, (CHUNK, CHUNK), 0)
    col_i = lax.broadcasted_iota(jnp.int32, (CHUNK, CHUNK), 1)
    causal = row_i >= col_i
    low = col_i < HALF

    for c in range(SSD_ROWS // CHUNK):
        r0 = c * CHUNK
        conv = cb_ref[...]
        for j in range(CONV_WIDTH):
            start = halo + r0 - (CONV_WIDTH - 1) + j
            conv = conv + cw_ref[j:j + 1, :] * xpad_ref[start:start + CHUNK, :]
        xc = _silu(conv)
        xs = xc[:, :SSD_WIDTH]
        bm = xc[:, SSD_WIDTH:SSD_WIDTH + N_GROUPS * D_STATE].astype(BF16)
        cm = xc[:, SSD_WIDTH + N_GROUPS * D_STATE:].astype(BF16)

        dt = _softplus(tail_ref[r0:r0 + CHUNK, :] + dtb_ref[...])
        adt = dt * a_ref[...]
        acs = _cumsum_rows(tri3, adt)
        acs_t = acs.T
        last = acs[CHUNK - 1:CHUNK, :]
        w = dt * jnp.exp(last - acs)
        ea = jnp.exp(acs)
        cd = jnp.broadcast_to(jnp.exp(last), (SUBLANES, LANES))
        stack = jnp.concatenate([dt, w, ea, cd], axis=0)
        lhs = jnp.concatenate(_split_bf16(stack, 2), axis=1)
        spread = jnp.dot(lhs, e2_ref[...], preferred_element_type=F32)
        dt_full = spread[0:CHUNK]
        w_full = spread[CHUNK:2 * CHUNK]
        ea_full = spread[2 * CHUNK:3 * CHUNK]
        cd_full = spread[3 * CHUNK:3 * CHUNK + 1]

        xdt = (xs * dt_full).astype(BF16)
        wx = (xs * w_full).astype(BF16)

        y_blocks = []
        yoff_blocks = []
        for g in range(N_GROUPS):
            gs = slice(g * D_STATE, (g + 1) * D_STATE)
            cb_g = lax.dot_general(cm[:, gs], bm[:, gs], (((1,), (1,)), ((), ())),
                                   preferred_element_type=F32)
            for jp in range(HEADS_PER_GROUP // 2):
                pair = g * (HEADS_PER_GROUP // 2) + jp
                x_pair = xdt[:, pair * LANES:(pair + 1) * LANES]
                ys = []
                for r in (2 * pair, 2 * pair + 1):
                    seg = acs[:, r:r + 1] - acs_t[r:r + 1, :]
                    dec = jnp.exp(jnp.where(causal, seg, -jnp.inf))
                    m = (cb_g * dec).astype(BF16)
                    ys.append(jnp.dot(m, x_pair, preferred_element_type=F32))
                y_blocks.append(jnp.where(low, ys[0], ys[1]))
            hs = slice(g * GROUP_WIDTH, (g + 1) * GROUP_WIDTH)
            st = state_ref[:, hs]
            yoff_blocks.append(jnp.dot(cm[:, gs], st.astype(BF16), preferred_element_type=F32))
            s_g = lax.dot_general(bm[:, gs], wx[:, hs], (((0,), (0,)), ((), ())),
                                  preferred_element_type=F32)
            state_ref[:, hs] = st * cd_full[:, hs] + s_g
        y = jnp.concatenate(y_blocks, axis=1)
        y_off = jnp.concatenate(yoff_blocks, axis=1)
        y = y + y_off * ea_full + dfull_ref[...] * xs
        y = y * _silu(z_ref[r0:r0 + CHUNK, :].astype(F32))
        outs = []
        for g in range(N_GROUPS):
            yg = y[:, g * GROUP_WIDTH:(g + 1) * GROUP_WIDTH]
            ms = jnp.mean(yg * yg, axis=-1, keepdims=True)
            outs.append(yg * lax.rsqrt(ms + EPS))
        yn = jnp.concatenate(outs, axis=1) * g_ref[...]
        y_ref[r0:r0 + CHUNK, :] = yn.astype(y_ref.dtype)

    xpad_ref[0:halo, :] = xpad_ref[SSD_ROWS:SSD_ROWS + halo, :]


def _ssd(xbc, tail, z_ssd, conv_w, conv_b, dtb, a_neg, d_full, g, bsz, seqlen):
    nt = seqlen // SSD_ROWS
    tri3 = _tri3()
    e2 = _head_expand2()
    row = lambda w: pl.BlockSpec((SSD_ROWS, w), lambda b, t: (b * nt + t, 0))
    full = _const_spec
    return pl.pallas_call(
        _ssd_kernel,
        out_shape=jax.ShapeDtypeStruct((bsz * seqlen, SSD_WIDTH), BF16),
        grid=(bsz, nt),
        in_specs=[row(CONV_CH), row(LANES), row(SSD_WIDTH), full(conv_w), full(conv_b),
                  full(dtb), full(a_neg), full(d_full), full(g), full(tri3), full(e2)],
        out_specs=row(SSD_WIDTH),
        scratch_shapes=[pltpu.VMEM((SSD_ROWS + 2 * SUBLANES, CONV_CH), F32),
                        pltpu.VMEM((D_STATE, SSD_WIDTH), F32)],
        compiler_params=pltpu.CompilerParams(
            dimension_semantics=("arbitrary", "arbitrary"), vmem_limit_bytes=VMEM_LIMIT),
        name="ssd",
    )(xbc, tail, z_ssd, conv_w, conv_b, dtb, a_neg, d_full, g, tri3, e2)


def _fox_attn_kernel(qa_ref, ka_ref, va_ref, z_ref, g_ref, o_ref, s_ref, acc_ref, m_ref):
    qi = pl.program_id(2)
    lane = lax.broadcasted_iota(jnp.int32, (ATT_TQ, LANES), 1)
    low = lane < HALF
    nt_dims = (((1,), (1,)), ((), ()))
    n_diag = ATT_TQ // ATT_TK
    all_rows = (0, ATT_TQ)

    def scores(hh, kstart, rows=all_rows):
        k = ka_ref[0, hh, pl.ds(kstart, ATT_TK), :]
        s_ref[hh, rows[0]:rows[1], :] = lax.dot_general(
            qa_ref[0, hh, rows[0]:rows[1], :], k, nt_dims, preferred_element_type=F32)

    def softmax_pv(hh, kstart, rows=all_rows, diag=None):
        r0, r1 = rows
        s = s_ref[hh, r0:r1, :]
        if diag is not None:
            row_i = lax.broadcasted_iota(jnp.int32, (r1 - r0, ATT_TK), 0) + r0
            col_i = lax.broadcasted_iota(jnp.int32, (r1 - r0, ATT_TK), 1) + diag * ATT_TK
            s = jnp.where(row_i >= col_i, s, -jnp.inf)
        m_old = m_ref[hh, r0:r1, :]
        m_new = jnp.maximum(m_old, jnp.max(s, axis=-1, keepdims=True))
        p = jnp.exp2(s - jnp.tile(m_new, (1, ATT_TK // LANES))).astype(BF16)
        alpha = jnp.exp2(m_old - m_new)
        m_ref[hh, r0:r1, :] = m_new
        v = va_ref[0, hh, pl.ds(kstart, ATT_TK), :]
        acc_ref[hh, r0:r1, :] = (alpha * acc_ref[hh, r0:r1, :]
                                 + jnp.dot(p, v, preferred_element_type=F32))

    m_ref[...] = jnp.full(m_ref.shape, NEG_BIG, F32)
    acc_ref[...] = jnp.zeros(acc_ref.shape, F32)
    scores(0, 0)

    def body(j, carry):
        for u in range(n_diag):
            kstart = pl.multiple_of(j * ATT_TQ, ATT_TQ) + u * ATT_TK
            scores(1, kstart)
            softmax_pv(0, kstart)
            scores(0, kstart + ATT_TK)
            softmax_pv(1, kstart)
        return carry

    lax.fori_loop(0, qi, body, 0)
    kdiag = pl.multiple_of(qi * ATT_TQ, ATT_TQ)
    for d in range(n_diag):
        rows = (d * ATT_TK, ATT_TQ)
        scores(1, kdiag + d * ATT_TK, rows)
        softmax_pv(0, kdiag + d * ATT_TK, rows, diag=d)
        if d + 1 < n_diag:
            scores(0, kdiag + (d + 1) * ATT_TK, ((d + 1) * ATT_TK, ATT_TQ))
        softmax_pv(1, kdiag + d * ATT_TK, rows, diag=d)

    def normed(hh):
        acc = acc_ref[hh]
        l_lane = 0 if hh else HALF
        mine = jnp.logical_not(low) if hh else low
        l = jnp.sum(jnp.where(lane == l_lane, acc, 0.0), axis=-1, keepdims=True)
        ssq = jnp.sum(jnp.where(mine, acc * acc, 0.0), axis=-1, keepdims=True)
        inv_l = 1.0 / l
        return acc * (inv_l * lax.rsqrt(ssq * (inv_l * inv_l) * (1.0 / ATT_HEAD_DIM) + EPS))

    att = jnp.where(low, normed(0), normed(1)) * g_ref[...]
    o_ref[...] = (att * _silu(z_ref[...].astype(F32))).astype(o_ref.dtype)


def _fox_attn(qa, ka, va, z_att, g2):
    assert ATT_TQ % ATT_TK == 0
    bsz, _, seqlen, _ = qa.shape
    nq = seqlen // ATT_TQ
    q_spec = pl.BlockSpec((1, 2, ATT_TQ, LANES), lambda b, p, i: (b, p, i, 0))
    kv_spec = pl.BlockSpec((1, 2, seqlen, LANES), lambda b, p, i: (b, p, 0, 0))
    row_spec = pl.BlockSpec((ATT_TQ, LANES), lambda b, p, i: (b * nq + i, p))
    return pl.pallas_call(
        _fox_attn_kernel,
        out_shape=jax.ShapeDtypeStruct((bsz * seqlen, ATT_WIDTH), BF16),
        grid=(bsz, ATT_HEADS // 2, nq),
        in_specs=[q_spec, kv_spec, kv_spec, row_spec,
                  _const_spec(g2)],
        out_specs=row_spec,
        scratch_shapes=[pltpu.VMEM((2, ATT_TQ, ATT_TK), F32),
                        pltpu.VMEM((2, ATT_TQ, LANES), F32),
                        pltpu.VMEM((2, ATT_TQ, LANES), F32)],
        compiler_params=pltpu.CompilerParams(
            dimension_semantics=("arbitrary", "arbitrary", "arbitrary"),
            vmem_limit_bytes=VMEM_LIMIT),
        name="fox_attn",
    )(qa, ka, va, z_att, g2)


def _rms(h, g):
    ms = jnp.mean(h * h, axis=-1, keepdims=True)
    return h * lax.rsqrt(ms + EPS) * g


def _out_ple_kernel(x_ref, ys_ref, ya_ref, p_ref, wo_s_ref, wo_a_ref, wg_ref, wp_ref,
                    gp_ref, gf_ref, o_ref):
    h = x_ref[...]
    h = h + (jnp.dot(ys_ref[...], wo_s_ref[...], preferred_element_type=F32)
             + jnp.dot(ya_ref[...], wo_a_ref[...], preferred_element_type=F32))
    hn = _rms(h, gp_ref[...]).astype(BF16)
    gate = jax.nn.sigmoid(jnp.dot(hn, wg_ref[...], preferred_element_type=F32))
    emb = jnp.dot(p_ref[...].astype(BF16), wp_ref[...], preferred_element_type=F32)
    h = h + gate * emb
    o_ref[...] = _rms(h, gf_ref[...])


def _out_ple(x2, y_ssd, y_att, p2, wo_s, wo_a, wg, wp, gp, gf):
    n = x2.shape[0]
    row = lambda w: pl.BlockSpec((OUT_ROWS, w), lambda i: (i, 0))
    full = _const_spec
    return pl.pallas_call(
        _out_ple_kernel,
        out_shape=jax.ShapeDtypeStruct((n, D_MODEL), F32),
        grid=(n // OUT_ROWS,),
        in_specs=[row(D_MODEL), row(SSD_WIDTH), row(ATT_WIDTH), row(PLE_DIM),
                  full(wo_s), full(wo_a), full(wg), full(wp), full(gp), full(gf)],
        out_specs=row(D_MODEL),
        compiler_params=pltpu.CompilerParams(
            dimension_semantics=("arbitrary",), vmem_limit_bytes=VMEM_LIMIT),
        name="out_ple",
    )(x2, y_ssd, y_att, p2, wo_s, wo_a, wg, wp, gp, gf)


def _layer(h2, p2, bsz, seqlen, norm_g, w_in, conv_w, conv_b, dt_bias, a_log, d_skip,
           ssd_norm_g, fg_bias, att_norm_g, w_out, ple_norm_g, w_ple_gate, w_ple_proj, out_g):
    o_zs = 0
    o_xbc = o_zs + SSD_WIDTH
    o_dt = o_xbc + CONV_CH
    o_za = o_dt + SSD_HEADS
    o_q = o_za + ATT_WIDTH
    o_k = o_q + ATT_WIDTH
    o_v = o_k + ATT_WIDTH
    o_f = o_v + ATT_WIDTH
    w_main = jnp.concatenate(
        [w_in[:, o_zs:o_dt], w_in[:, o_za:o_f]], axis=1).astype(BF16)
    w_tail = jnp.concatenate(
        [w_in[:, o_dt:o_za], w_in[:, o_f:o_f + ATT_HEADS],
         jnp.zeros((D_MODEL, LANES - SSD_HEADS - ATT_HEADS), w_in.dtype)], axis=1).astype(BF16)

    def lanes16(v, off):
        return jnp.zeros((1, LANES), F32).at[0, off:off + v.shape[0]].set(v.astype(F32))

    z_ssd, xbc, z_att, qa, ka, va, tail = _in_proj(
        h2, norm_g.reshape(1, D_MODEL).astype(F32), w_main, w_tail,
        lanes16(fg_bias, TAIL_F), bsz, seqlen)

    y_ssd = _ssd(
        xbc, tail, z_ssd, conv_w.astype(F32), conv_b.reshape(1, CONV_CH).astype(F32),
        lanes16(dt_bias, TAIL_DT), lanes16(-jnp.exp(a_log.astype(F32)), TAIL_DT),
        jnp.repeat(d_skip.astype(F32), SSD_HEAD_DIM).reshape(1, SSD_WIDTH),
        ssd_norm_g.reshape(1, SSD_WIDTH).astype(F32), bsz, seqlen)

    y_att = _fox_attn(qa, ka, va, z_att,
                      jnp.tile(att_norm_g.astype(F32), 2).reshape(1, LANES))

    return _out_ple(
        h2, y_ssd, y_att, p2,
        w_out[:SSD_WIDTH].astype(BF16), w_out[SSD_WIDTH:].astype(BF16),
        w_ple_gate.astype(BF16), w_ple_proj.astype(BF16),
        ple_norm_g.reshape(1, D_MODEL).astype(F32), out_g.reshape(1, D_MODEL).astype(F32))


def kernel(x, p, norm_g, w_in, conv_w, conv_b, dt_bias, a_log, d_skip, ssd_norm_g, fg_bias,
           att_norm_g, w_out, ple_norm_g, w_ple_gate, w_ple_proj, final_norm_g):
    bsz, seqlen, _ = x.shape
    depth = p.shape[0]
    assert depth == 1, "the fused tail applies the final norm right after the only layer"
    h2 = x.reshape(bsz * seqlen, D_MODEL)
    out = _layer(h2, p[0].reshape(bsz * seqlen, PLE_DIM), bsz, seqlen,
                 norm_g[0], w_in[0], conv_w[0], conv_b[0], dt_bias[0], a_log[0], d_skip[0],
                 ssd_norm_g[0], fg_bias[0], att_norm_g[0], w_out[0], ple_norm_g[0],
                 w_ple_gate[0], w_ple_proj[0], final_norm_g)
    return out.reshape(bsz, seqlen, D_MODEL)
```

```python
import functools
import math

import jax
import jax.numpy as jnp
import numpy as np
from jax import lax
from jax.experimental import pallas as pl
from jax.experimental.pallas import tpu as pltpu

F32 = jnp.float32
BF16 = jnp.bfloat16

D_MODEL = 1024
SSD_WIDTH = 1024
ATT_WIDTH = 1024
SSD_HEAD_DIM = 64
SSD_HEADS = 16
N_GROUPS = 2
HEADS_PER_GROUP = SSD_HEADS // N_GROUPS
GROUP_WIDTH = SSD_WIDTH // N_GROUPS
D_STATE = 128
CONV_WIDTH = 4
CHUNK = 128
ATT_HEAD_DIM = 64
ATT_HEADS = 16
PLE_DIM = 256
EPS = 1e-6
CONV_CH = SSD_WIDTH + 2 * N_GROUPS * D_STATE

LANES = 128
SUBLANES = 8
HALF = LANES // 2
BF16_ROWS = 16
CONV_HALO = 128

LOG2E = 1.4426950408889634
NEG_BIG = -1e30

SEC_ZS = (0, SSD_WIDTH)
SEC_XBC = (SEC_ZS[0] + SEC_ZS[1], CONV_CH)
SEC_ZA = (SEC_XBC[0] + SEC_XBC[1], ATT_WIDTH)
SEC_Q = (SEC_ZA[0] + SEC_ZA[1], ATT_WIDTH)
SEC_K = (SEC_Q[0] + SEC_Q[1], ATT_WIDTH)
SEC_V = (SEC_K[0] + SEC_K[1], ATT_WIDTH)
MAIN_COLS = SEC_V[0] + SEC_V[1]
TAIL_DT = 0
TAIL_F = SSD_HEADS
SPLIT_PARTS = 3
TAIL_ONE = TAIL_DT

IN_ROWS = 512
IN_COL_CHUNK = 512
SSD_ROWS = 256
ATT_TQ = 1024
ATT_TK = 512
OUT_ROWS = 512
VMEM_LIMIT = 56 * 1024 * 1024


def _split_bf16(x, parts):
    out = []
    r = x
    for i in range(parts):
        h = r.astype(BF16)
        out.append(h)
        if i + 1 < parts:
            r = r - h.astype(F32)
    return out


def _softplus(x):
    return jnp.maximum(x, 0.0) + jnp.log1p(jnp.exp(-jnp.abs(x)))


def _silu(x):
    return x * jax.nn.sigmoid(x)


def _aug_lane(head, j):
    pair, odd = divmod(head, 2)
    return pair * LANES + (j if odd else HALF + j)


def _aug_matrices():
    eq = np.zeros((LANES, ATT_WIDTH), np.float32)
    ek = np.zeros((LANES, ATT_WIDTH), np.float32)
    for h in range(ATT_HEADS):
        for part in range(SPLIT_PARTS):
            src = TAIL_F + part * ATT_HEADS + h
            eq[src, _aug_lane(h, part)] = 1.0
            ek[src, _aug_lane(h, SPLIT_PARTS + part)] = -1.0
            eq[TAIL_ONE, _aug_lane(h, SPLIT_PARTS + part)] = 1.0
            ek[TAIL_ONE, _aug_lane(h, part)] = 1.0
    return jnp.asarray(eq, BF16), jnp.asarray(ek, BF16)


def _tri3():
    tri = np.tril(np.ones((CHUNK, CHUNK), np.float32))
    return jnp.asarray(np.concatenate([tri, tri, tri], axis=1), BF16)


def _cumsum_rows(tri3, v):
    parts = _split_bf16(v, 3)
    return jnp.dot(tri3, jnp.concatenate(parts, axis=0), preferred_element_type=F32)


def _in_proj_kernel(x_ref, g_ref, wm_ref, wt_ref, fgb_ref, tri3_ref, eq_ref, ek_ref,
                    zs_ref, xbc_ref, za_ref, qa_ref, ka_ref, va_ref, tail_ref, carry_ref):
    @pl.when(pl.program_id(1) == 0)
    def _():
        carry_ref[...] = jnp.zeros_like(carry_ref)

    x = x_ref[...]
    ms = jnp.mean(x * x, axis=-1, keepdims=True)
    u = (x * lax.rsqrt(ms + EPS) * g_ref[...]).astype(BF16)

    def project(sec, c):
        return jnp.dot(u, wm_ref[:, sec[0] + c:sec[0] + c + IN_COL_CHUNK],
                       preferred_element_type=F32)

    def section(out_ref, sec):
        for c in range(0, sec[1], IN_COL_CHUNK):
            out_ref[:, c:c + IN_COL_CHUNK] = project(sec, c).astype(out_ref.dtype)

    tail = jnp.dot(u, wt_ref[...], preferred_element_type=F32)
    tail_ref[...] = tail

    tri3 = tri3_ref[...]
    part_lane = lax.broadcasted_iota(jnp.int32, (CHUNK, LANES), 1)
    carry = carry_ref[...]
    lhs = []
    for c in range(IN_ROWS // CHUNK):
        log_f = -_softplus(-(tail[c * CHUNK:(c + 1) * CHUNK] + fgb_ref[...]))
        cum = _cumsum_rows(tri3, log_f) + carry
        carry = cum[CHUNK - 1:CHUNK, :]
        hi, mid, lo = _split_bf16(cum * LOG2E, SPLIT_PARTS)
        parts = jnp.where(part_lane < TAIL_F + ATT_HEADS, hi,
                          jnp.where(part_lane < TAIL_F + 2 * ATT_HEADS, mid, lo))
        lhs.append(jnp.where(part_lane == TAIL_ONE, jnp.ones_like(parts), parts))
    carry_ref[...] = carry
    lhs = jnp.concatenate(lhs, axis=0)
    aq = jnp.dot(lhs, eq_ref[...], preferred_element_type=F32).astype(BF16)
    ak = jnp.dot(lhs, ek_ref[...], preferred_element_type=F32).astype(BF16)

    section(zs_ref, SEC_ZS)
    section(xbc_ref, SEC_XBC)
    section(za_ref, SEC_ZA)

    lane = lax.broadcasted_iota(jnp.int32, (IN_ROWS, LANES), 1)
    low = lane < HALF
    one_even = jnp.where(lane == HALF, 1.0, 0.0).astype(BF16)
    one_odd = jnp.where(lane == 0, 1.0, 0.0).astype(BF16)

    def head_major(out_ref, sec, aug_even, aug_odd, mul=None):
        for c in range(0, sec[1], IN_COL_CHUNK):
            acc = project(sec, c)
            if mul is not None:
                acc = acc * mul
            acc = acc.astype(BF16)
            for i in range(IN_COL_CHUNK // LANES):
                pair = c // LANES + i
                d2 = acc[:, i * LANES:(i + 1) * LANES]
                out_ref[0, 2 * pair] = jnp.where(low, d2, aug_even(pair))
                out_ref[0, 2 * pair + 1] = jnp.where(low, aug_odd(pair), d2)

    pair_cols = lambda a: (lambda pair: a[:, pair * LANES:(pair + 1) * LANES])
    head_major(qa_ref, SEC_Q, pair_cols(aq), pair_cols(aq), mul=ATT_HEAD_DIM ** -0.5 * LOG2E)
    head_major(ka_ref, SEC_K, pair_cols(ak), pair_cols(ak))
    head_major(va_ref, SEC_V, lambda pair: one_even, lambda pair: one_odd)


def _const_spec(a):
    zeros = (0,) * a.ndim
    return pl.BlockSpec(a.shape, lambda *_: zeros, pipeline_mode=pl.Buffered(1))


def _in_proj(x2, g, w_main, w_tail, fgb, bsz, seqlen):
    n = x2.shape[0]
    nt = seqlen // IN_ROWS
    tri3 = _tri3()
    eq, ek = _aug_matrices()
    row = lambda w: pl.BlockSpec((IN_ROWS, w), lambda b, t: (b * nt + t, 0))
    full = _const_spec
    head_major = pl.BlockSpec((1, ATT_HEADS, IN_ROWS, LANES), lambda b, t: (b, 0, t, 0))
    aug_shape = jax.ShapeDtypeStruct((bsz, ATT_HEADS, seqlen, LANES), BF16)
    out_shapes = (
        jax.ShapeDtypeStruct((n, SSD_WIDTH), BF16),
        jax.ShapeDtypeStruct((n, CONV_CH), BF16),
        jax.ShapeDtypeStruct((n, ATT_WIDTH), BF16),
        aug_shape, aug_shape, aug_shape,
        jax.ShapeDtypeStruct((n, LANES), F32),
    )
    return pl.pallas_call(
        _in_proj_kernel,
        out_shape=out_shapes,
        grid=(bsz, nt),
        in_specs=[row(D_MODEL), full(g), full(w_main), full(w_tail), full(fgb),
                  full(tri3), full(eq), full(ek)],
        out_specs=(row(SSD_WIDTH), row(CONV_CH), row(ATT_WIDTH),
                   head_major, head_major, head_major, row(LANES)),
        scratch_shapes=[pltpu.VMEM((1, LANES), F32)],
        compiler_params=pltpu.CompilerParams(
            dimension_semantics=("arbitrary", "arbitrary"), vmem_limit_bytes=VMEM_LIMIT),
        name="in_proj",
    )(x2, g, w_main, w_tail, fgb, tri3, eq, ek)


def _head_expand2():
    e = np.zeros((2 * LANES, SSD_WIDTH), np.float32)
    for h in range(SSD_HEADS):
        e[TAIL_DT + h, h * SSD_HEAD_DIM:(h + 1) * SSD_HEAD_DIM] = 1.0
        e[LANES + TAIL_DT + h, h * SSD_HEAD_DIM:(h + 1) * SSD_HEAD_DIM] = 1.0
    return jnp.asarray(e, BF16)


def _conv_shift():
    s = np.zeros((CONV_WIDTH * CHUNK, CONV_HALO + CHUNK), np.float32)
    for j in range(CONV_WIDTH):
        for l in range(CHUNK):
            s[j * CHUNK + l, CONV_HALO + l - (CONV_WIDTH - 1) + j] = 1.0
    return jnp.asarray(s, BF16)


def _ssd_kernel(xbc_ref, tail_ref, z_ref, cw_ref, cb_ref, dtb_ref, a_ref, dfull_ref, g_ref,
                tri3_ref, e2_ref, shift_ref, xprev_ref, y_ref, xwin_ref, state_ref):
    @pl.when(pl.program_id(1) == 0)
    def _():
        state_ref[...] = jnp.zeros_like(state_ref)

    prev = xprev_ref[...]
    xwin_ref[0:CONV_HALO, :] = jnp.where(pl.program_id(1) > 0, prev, jnp.zeros_like(prev))
    xwin_ref[CONV_HALO:CONV_HALO + SSD_ROWS, :] = xbc_ref[...]

    tri3 = tri3_ref[...]
    row_i = lax.broadcasted_iota(jnp.int32, (CHUNK, CHUNK), 0)
    col_i = lax.broadcasted_iota(jnp.int32, (CHUNK, CHUNK), 1)
    causal = row_i >= col_i
    low = col_i < HALF

    for c in range(SSD_ROWS // CHUNK):
        r0 = c * CHUNK
        window = xwin_ref[r0:r0 + CONV_HALO + CHUNK, :]
        taps = jnp.dot(shift_ref[...], window, preferred_element_type=F32)
        conv = cb_ref[...]
        for j in range(CONV_WIDTH):
            conv = conv + cw_ref[j:j + 1, :] * taps[j * CHUNK:(j + 1) * CHUNK]
        xc = _silu(conv)
        xs = xc[:, :SSD_WIDTH]
        bm = xc[:, SSD_WIDTH:SSD_WIDTH + N_GROUPS * D_STATE].astype(BF16)
        cm = xc[:, SSD_WIDTH + N_GROUPS * D_STATE:].astype(BF16)

        dt = _softplus(tail_ref[r0:r0 + CHUNK, :] + dtb_ref[...])
        adt = dt * a_ref[...]
        acs = _cumsum_rows(tri3, adt)
        acs_t = acs.T
        last = acs[CHUNK - 1:CHUNK, :]
        w = dt * jnp.exp(last - acs)
        ea = jnp.exp(acs)
        cd = jnp.broadcast_to(jnp.exp(last), (BF16_ROWS, LANES))
        stack = jnp.concatenate([dt, w, ea, cd], axis=0)
        lhs = jnp.concatenate(_split_bf16(stack, 2), axis=1)
        spread = jnp.dot(lhs, e2_ref[...], preferred_element_type=F32)
        dt_full = spread[0:CHUNK]
        w_full = spread[CHUNK:2 * CHUNK]
        ea_full = spread[2 * CHUNK:3 * CHUNK]
        cd_full = spread[3 * CHUNK:3 * CHUNK + 1]

        xdt = (xs * dt_full).astype(BF16)
        wx = (xs * w_full).astype(BF16)

        y_blocks = []
        yoff_blocks = []
        for g in range(N_GROUPS):
            gs = slice(g * D_STATE, (g + 1) * D_STATE)
            cb_g = lax.dot_general(cm[:, gs], bm[:, gs], (((1,), (1,)), ((), ())),
                                   preferred_element_type=F32)
            for jp in range(HEADS_PER_GROUP // 2):
                pair = g * (HEADS_PER_GROUP // 2) + jp
                x_pair = xdt[:, pair * LANES:(pair + 1) * LANES]
                ys = []
                for r in (2 * pair, 2 * pair + 1):
                    seg = acs[:, r:r + 1] - acs_t[r:r + 1, :]
                    dec = jnp.exp(jnp.where(causal, seg, -jnp.inf))
                    m = (cb_g * dec).astype(BF16)
                    ys.append(jnp.dot(m, x_pair, preferred_element_type=F32))
                y_blocks.append(jnp.where(low, ys[0], ys[1]))
            hs = slice(g * GROUP_WIDTH, (g + 1) * GROUP_WIDTH)
            st = state_ref[:, hs]
            yoff_blocks.append(jnp.dot(cm[:, gs], st.astype(BF16), preferred_element_type=F32))
            s_g = lax.dot_general(bm[:, gs], wx[:, hs], (((0,), (0,)), ((), ())),
                                  preferred_element_type=F32)
            state_ref[:, hs] = st * cd_full[:, hs] + s_g
        y = jnp.concatenate(y_blocks, axis=1)
        y_off = jnp.concatenate(yoff_blocks, axis=1)
        y = y + y_off * ea_full + dfull_ref[...] * xs
        y = y * _silu(z_ref[r0:r0 + CHUNK, :].astype(F32))
        outs = []
        for g in range(N_GROUPS):
            yg = y[:, g * GROUP_WIDTH:(g + 1) * GROUP_WIDTH]
            ms = jnp.mean(yg * yg, axis=-1, keepdims=True)
            outs.append(yg * lax.rsqrt(ms + EPS))
        yn = jnp.concatenate(outs, axis=1) * g_ref[...]
        y_ref[r0:r0 + CHUNK, :] = yn.astype(y_ref.dtype)


def _ssd(xbc, tail, z_ssd, conv_w, conv_b, dtb, a_neg, d_full, g, bsz, seqlen):
    nt = seqlen // SSD_ROWS
    tri3 = _tri3()
    e2 = _head_expand2()
    shift = _conv_shift()
    row = lambda w: pl.BlockSpec((SSD_ROWS, w), lambda b, t: (b * nt + t, 0))
    full = _const_spec
    halo_per_tile = SSD_ROWS // CONV_HALO
    prev = pl.BlockSpec(
        (CONV_HALO, CONV_CH),
        lambda b, t: (jnp.maximum((b * nt + t) * halo_per_tile - 1, 0), 0))
    return pl.pallas_call(
        _ssd_kernel,
        out_shape=jax.ShapeDtypeStruct((bsz * seqlen, SSD_WIDTH), BF16),
        grid=(bsz, nt),
        in_specs=[row(CONV_CH), row(LANES), row(SSD_WIDTH), full(conv_w), full(conv_b),
                  full(dtb), full(a_neg), full(d_full), full(g), full(tri3), full(e2),
                  full(shift), prev],
        out_specs=row(SSD_WIDTH),
        scratch_shapes=[pltpu.VMEM((CONV_HALO + SSD_ROWS, CONV_CH), BF16),
                        pltpu.VMEM((D_STATE, SSD_WIDTH), F32)],
        compiler_params=pltpu.CompilerParams(
            dimension_semantics=("arbitrary", "arbitrary"), vmem_limit_bytes=VMEM_LIMIT),
        name="ssd",
    )(xbc, tail, z_ssd, conv_w, conv_b, dtb, a_neg, d_full, g, tri3, e2, shift, xbc)


def _fox_attn_kernel(qa_ref, ka_ref, va_ref, z_ref, g_ref, o_ref, s_ref, acc_ref, m_ref):
    qi = pl.program_id(2)
    lane = lax.broadcasted_iota(jnp.int32, (ATT_TQ, LANES), 1)
    low = lane < HALF
    nt_dims = (((1,), (1,)), ((), ()))
    n_diag = ATT_TQ // ATT_TK
    all_rows = (0, ATT_TQ)

    def scores(hh, kstart, rows=all_rows):
        k = ka_ref[0, hh, pl.ds(kstart, ATT_TK), :]
        s_ref[hh, rows[0]:rows[1], :] = lax.dot_general(
            qa_ref[0, hh, rows[0]:rows[1], :], k, nt_dims, preferred_element_type=F32)

    def softmax_pv(hh, kstart, rows=all_rows, diag=None):
        r0, r1 = rows
        s = s_ref[hh, r0:r1, :]
        if diag is not None:
            row_i = lax.broadcasted_iota(jnp.int32, (r1 - r0, ATT_TK), 0) + r0
            col_i = lax.broadcasted_iota(jnp.int32, (r1 - r0, ATT_TK), 1) + diag * ATT_TK
            s = jnp.where(row_i >= col_i, s, -jnp.inf)
        m_old = m_ref[hh, r0:r1, :]
        m_new = jnp.maximum(m_old, jnp.max(s, axis=-1, keepdims=True))
        p = jnp.exp2(s - jnp.tile(m_new, (1, ATT_TK // LANES))).astype(BF16)
        alpha = jnp.exp2(m_old - m_new)
        m_ref[hh, r0:r1, :] = m_new
        v = va_ref[0, hh, pl.ds(kstart, ATT_TK), :]
        acc_ref[hh, r0:r1, :] = (alpha * acc_ref[hh, r0:r1, :]
                                 + jnp.dot(p, v, preferred_element_type=F32))

    m_ref[...] = jnp.full(m_ref.shape, NEG_BIG, F32)
    acc_ref[...] = jnp.zeros(acc_ref.shape, F32)
    scores(0, 0)

    def body(j, carry):
        for u in range(n_diag):
            kstart = pl.multiple_of(j * ATT_TQ, ATT_TQ) + u * ATT_TK
            scores(1, kstart)
            softmax_pv(0, kstart)
            scores(0, kstart + ATT_TK)
            softmax_pv(1, kstart)
        return carry

    lax.fori_loop(0, qi, body, 0)
    kdiag = pl.multiple_of(qi * ATT_TQ, ATT_TQ)
    for d in range(n_diag):
        rows = (d * ATT_TK, ATT_TQ)
        scores(1, kdiag + d * ATT_TK, rows)
        softmax_pv(0, kdiag + d * ATT_TK, rows, diag=d)
        if d + 1 < n_diag:
            scores(0, kdiag + (d + 1) * ATT_TK, ((d + 1) * ATT_TK, ATT_TQ))
        softmax_pv(1, kdiag + d * ATT_TK, rows, diag=d)

    def normed(hh):
        acc = acc_ref[hh]
        l_lane = 0 if hh else HALF
        mine = jnp.logical_not(low) if hh else low
        l = jnp.sum(jnp.where(lane == l_lane, acc, 0.0), axis=-1, keepdims=True)
        ssq = jnp.sum(jnp.where(mine, acc * acc, 0.0), axis=-1, keepdims=True)
        inv_l = 1.0 / l
        return acc * (inv_l * lax.rsqrt(ssq * (inv_l * inv_l) * (1.0 / ATT_HEAD_DIM) + EPS))

    att = jnp.where(low, normed(0), normed(1)) * g_ref[...]
    o_ref[...] = (att * _silu(z_ref[...].astype(F32))).astype(o_ref.dtype)


def _fox_attn(qa, ka, va, z_att, g2):
    assert ATT_TQ % ATT_TK == 0
    bsz, _, seqlen, _ = qa.shape
    nq = seqlen // ATT_TQ
    q_spec = pl.BlockSpec((1, 2, ATT_TQ, LANES), lambda b, p, i: (b, p, i, 0))
    kv_spec = pl.BlockSpec((1, 2, seqlen, LANES), lambda b, p, i: (b, p, 0, 0))
    row_spec = pl.BlockSpec((ATT_TQ, LANES), lambda b, p, i: (b * nq + i, p))
    return pl.pallas_call(
        _fox_attn_kernel,
        out_shape=jax.ShapeDtypeStruct((bsz * seqlen, ATT_WIDTH), BF16),
        grid=(bsz, ATT_HEADS // 2, nq),
        in_specs=[q_spec, kv_spec, kv_spec, row_spec,
                  _const_spec(g2)],
        out_specs=row_spec,
        scratch_shapes=[pltpu.VMEM((2, ATT_TQ, ATT_TK), F32),
                        pltpu.VMEM((2, ATT_TQ, LANES), F32),
                        pltpu.VMEM((2, ATT_TQ, LANES), F32)],
        compiler_params=pltpu.CompilerParams(
            dimension_semantics=("arbitrary", "arbitrary", "arbitrary"),
            vmem_limit_bytes=VMEM_LIMIT),
        name="fox_attn",
    )(qa, ka, va, z_att, g2)


def _rms(h, g):
    ms = jnp.mean(h * h, axis=-1, keepdims=True)
    return h * lax.rsqrt(ms + EPS) * g


def _out_ple_kernel(x_ref, ys_ref, ya_ref, p_ref, wo_s_ref, wo_a_ref, wg_ref, wp_ref,
                    gp_ref, gf_ref, o_ref):
    h = x_ref[...]
    h = h + (jnp.dot(ys_ref[...], wo_s_ref[...], preferred_element_type=F32)
             + jnp.dot(ya_ref[...], wo_a_ref[...], preferred_element_type=F32))
    hn = _rms(h, gp_ref[...]).astype(BF16)
    gate = jax.nn.sigmoid(jnp.dot(hn, wg_ref[...], preferred_element_type=F32))
    emb = jnp.dot(p_ref[...].astype(BF16), wp_ref[...], preferred_element_type=F32)
    h = h + gate * emb
    o_ref[...] = _rms(h, gf_ref[...])


def _out_ple(x2, y_ssd, y_att, p2, wo_s, wo_a, wg, wp, gp, gf):
    n = x2.shape[0]
    row = lambda w: pl.BlockSpec((OUT_ROWS, w), lambda i: (i, 0))
    full = _const_spec
    return pl.pallas_call(
        _out_ple_kernel,
        out_shape=jax.ShapeDtypeStruct((n, D_MODEL), F32),
        grid=(n // OUT_ROWS,),
        in_specs=[row(D_MODEL), row(SSD_WIDTH), row(ATT_WIDTH), row(PLE_DIM),
                  full(wo_s), full(wo_a), full(wg), full(wp), full(gp), full(gf)],
        out_specs=row(D_MODEL),
        compiler_params=pltpu.CompilerParams(
            dimension_semantics=("arbitrary",), vmem_limit_bytes=VMEM_LIMIT),
        name="out_ple",
    )(x2, y_ssd, y_att, p2, wo_s, wo_a, wg, wp, gp, gf)


def _layer(h2, p2, bsz, seqlen, norm_g, w_in, conv_w, conv_b, dt_bias, a_log, d_skip,
           ssd_norm_g, fg_bias, att_norm_g, w_out, ple_norm_g, w_ple_gate, w_ple_proj, out_g):
    o_zs = 0
    o_xbc = o_zs + SSD_WIDTH
    o_dt = o_xbc + CONV_CH
    o_za = o_dt + SSD_HEADS
    o_q = o_za + ATT_WIDTH
    o_k = o_q + ATT_WIDTH
    o_v = o_k + ATT_WIDTH
    o_f = o_v + ATT_WIDTH
    w_in16 = w_in.astype(BF16)
    w_main = jnp.concatenate([w_in16[:, o_zs:o_dt], w_in16[:, o_za:o_f]], axis=1)
    w_f = w_in16[:, o_f:o_f + ATT_HEADS]
    w_tail = jnp.concatenate(
        [w_in16[:, o_dt:o_za]] + [w_f] * SPLIT_PARTS
        + [jnp.zeros((D_MODEL, LANES - SSD_HEADS - SPLIT_PARTS * ATT_HEADS), BF16)], axis=1)

    def lanes16(v, off):
        return jnp.zeros((1, LANES), F32).at[0, off:off + v.shape[0]].set(v.astype(F32))

    z_ssd, xbc, z_att, qa, ka, va, tail = _in_proj(
        h2, norm_g.reshape(1, D_MODEL).astype(F32), w_main, w_tail,
        lanes16(jnp.tile(fg_bias, SPLIT_PARTS), TAIL_F), bsz, seqlen)

    y_ssd = _ssd(
        xbc, tail, z_ssd, conv_w.astype(F32), conv_b.reshape(1, CONV_CH).astype(F32),
        lanes16(dt_bias, TAIL_DT), lanes16(-jnp.exp(a_log.astype(F32)), TAIL_DT),
        jnp.repeat(d_skip.astype(F32), SSD_HEAD_DIM).reshape(1, SSD_WIDTH),
        ssd_norm_g.reshape(1, SSD_WIDTH).astype(F32), bsz, seqlen)

    y_att = _fox_attn(qa, ka, va, z_att,
                      jnp.tile(att_norm_g.astype(F32), 2).reshape(1, LANES))

    return _out_ple(
        h2, y_ssd, y_att, p2,
        w_out[:SSD_WIDTH].astype(BF16), w_out[SSD_WIDTH:].astype(BF16),
        w_ple_gate.astype(BF16), w_ple_proj.astype(BF16),
        ple_norm_g.reshape(1, D_MODEL).astype(F32), out_g.reshape(1, D_MODEL).astype(F32))


def kernel(x, p, norm_g, w_in, conv_w, conv_b, dt_bias, a_log, d_skip, ssd_norm_g, fg_bias,
           att_norm_g, w_out, ple_norm_g, w_ple_gate, w_ple_proj, final_norm_g):
    bsz, seqlen, _ = x.shape
    depth = p.shape[0]
    assert depth == 1, "the fused tail applies the final norm right after the only layer"
    h2 = x.reshape(bsz * seqlen, D_MODEL)
    out = _layer(h2, p[0].reshape(bsz * seqlen, PLE_DIM), bsz, seqlen,
                 norm_g[0], w_in[0], conv_w[0], conv_b[0], dt_bias[0], a_log[0], d_skip[0],
                 ssd_norm_g[0], fg_bias[0], att_norm_g[0], w_out[0], ple_norm_g[0],
                 w_ple_gate[0], w_ple_proj[0], final_norm_g)
    return out.reshape(bsz, seqlen, D_MODEL)
```

```python
import functools
import math

import jax
import jax.numpy as jnp
import numpy as np
from jax import lax
from jax.experimental import pallas as pl
from jax.experimental.pallas import tpu as pltpu

F32 = jnp.float32
BF16 = jnp.bfloat16

D_MODEL = 1024
SSD_WIDTH = 1024
ATT_WIDTH = 1024
SSD_HEAD_DIM = 64
SSD_HEADS = 16
N_GROUPS = 2
HEADS_PER_GROUP = SSD_HEADS // N_GROUPS
GROUP_WIDTH = SSD_WIDTH // N_GROUPS
D_STATE = 128
CONV_WIDTH = 4
CHUNK = 128
ATT_HEAD_DIM = 64
ATT_HEADS = 16
PLE_DIM = 256
EPS = 1e-6
CONV_CH = SSD_WIDTH + 2 * N_GROUPS * D_STATE

LANES = 128
SUBLANES = 8
HALF = LANES // 2
BF16_ROWS = 16
CONV_HALO = 128

LOG2E = 1.4426950408889634
NEG_BIG = -1e30
SKIP_MARGIN = 160.0
NORM_SLACK = 1.01

SEC_ZS = (0, SSD_WIDTH)
SEC_XBC = (SEC_ZS[0] + SEC_ZS[1], CONV_CH)
SEC_ZA = (SEC_XBC[0] + SEC_XBC[1], ATT_WIDTH)
SEC_Q = (SEC_ZA[0] + SEC_ZA[1], ATT_WIDTH)
SEC_K = (SEC_Q[0] + SEC_Q[1], ATT_WIDTH)
SEC_V = (SEC_K[0] + SEC_K[1], ATT_WIDTH)
MAIN_COLS = SEC_V[0] + SEC_V[1]
TAIL_DT = 0
TAIL_F = SSD_HEADS
SPLIT_PARTS = 3
TAIL_ONE = TAIL_DT

IN_ROWS = 512
IN_COL_CHUNK = 512
SSD_ROWS = 256
ATT_TQ = 1024
ATT_TK = 512
OUT_ROWS = 512
VMEM_LIMIT = 56 * 1024 * 1024


def _split_bf16(x, parts):
    out = []
    r = x
    for i in range(parts):
        h = r.astype(BF16)
        out.append(h)
        if i + 1 < parts:
            r = r - h.astype(F32)
    return out


def _softplus(x):
    return jnp.maximum(x, 0.0) + jnp.log1p(jnp.exp(-jnp.abs(x)))


def _silu(x):
    return x * jax.nn.sigmoid(x)


def _aug_lane(head, j):
    pair, odd = divmod(head, 2)
    return pair * LANES + (j if odd else HALF + j)


def _aug_matrices():
    eq = np.zeros((LANES, ATT_WIDTH), np.float32)
    ek = np.zeros((LANES, ATT_WIDTH), np.float32)
    for h in range(ATT_HEADS):
        for part in range(SPLIT_PARTS):
            src = TAIL_F + part * ATT_HEADS + h
            eq[src, _aug_lane(h, part)] = 1.0
            ek[src, _aug_lane(h, SPLIT_PARTS + part)] = -1.0
            eq[TAIL_ONE, _aug_lane(h, SPLIT_PARTS + part)] = 1.0
            ek[TAIL_ONE, _aug_lane(h, part)] = 1.0
    return jnp.asarray(eq, BF16), jnp.asarray(ek, BF16)


def _tri3():
    tri = np.tril(np.ones((CHUNK, CHUNK), np.float32))
    return jnp.asarray(np.concatenate([tri, tri, tri], axis=1), BF16)


def _cumsum_rows(tri3, v):
    parts = _split_bf16(v, 3)
    return jnp.dot(tri3, jnp.concatenate(parts, axis=0), preferred_element_type=F32)


def _in_proj_kernel(x_ref, g_ref, wm_ref, wt_ref, fgb_ref, tri3_ref, eq_ref, ek_ref,
                    zs_ref, xbc_ref, za_ref, qa_ref, ka_ref, va_ref, tail_ref, stats_ref,
                    carry_ref):
    @pl.when(pl.program_id(1) == 0)
    def _():
        carry_ref[...] = jnp.zeros_like(carry_ref)

    x = x_ref[...]
    ms = jnp.mean(x * x, axis=-1, keepdims=True)
    u = (x * lax.rsqrt(ms + EPS) * g_ref[...]).astype(BF16)

    def project(sec, c):
        return jnp.dot(u, wm_ref[:, sec[0] + c:sec[0] + c + IN_COL_CHUNK],
                       preferred_element_type=F32)

    def section(out_ref, sec):
        for c in range(0, sec[1], IN_COL_CHUNK):
            out_ref[:, c:c + IN_COL_CHUNK] = project(sec, c).astype(out_ref.dtype)

    tail = jnp.dot(u, wt_ref[...], preferred_element_type=F32)
    tail_ref[...] = tail

    tri3 = tri3_ref[...]
    part_lane = lax.broadcasted_iota(jnp.int32, (CHUNK, LANES), 1)
    carry = carry_ref[...]
    lhs = []
    for c in range(IN_ROWS // CHUNK):
        log_f = -_softplus(-(tail[c * CHUNK:(c + 1) * CHUNK] + fgb_ref[...]))
        cum = _cumsum_rows(tri3, log_f) + carry
        carry = cum[CHUNK - 1:CHUNK, :]
        if c == 0:
            c_first = cum[0:1, :] * LOG2E
        hi, mid, lo = _split_bf16(cum * LOG2E, SPLIT_PARTS)
        parts = jnp.where(part_lane < TAIL_F + ATT_HEADS, hi,
                          jnp.where(part_lane < TAIL_F + 2 * ATT_HEADS, mid, lo))
        lhs.append(jnp.where(part_lane == TAIL_ONE, jnp.ones_like(parts), parts))
    carry_ref[...] = carry
    lhs = jnp.concatenate(lhs, axis=0)
    aq = jnp.dot(lhs, eq_ref[...], preferred_element_type=F32).astype(BF16)
    ak = jnp.dot(lhs, ek_ref[...], preferred_element_type=F32).astype(BF16)

    section(zs_ref, SEC_ZS)
    section(xbc_ref, SEC_XBC)
    section(za_ref, SEC_ZA)

    lane = lax.broadcasted_iota(jnp.int32, (IN_ROWS, LANES), 1)
    low = lane < HALF
    one_even = jnp.where(lane == HALF, 1.0, 0.0).astype(BF16)
    one_odd = jnp.where(lane == 0, 1.0, 0.0).astype(BF16)

    head_lane = lax.broadcasted_iota(jnp.int32, (1, LANES), 1)

    def head_major(out_ref, sec, aug_even, aug_odd, mul=None, norms=False):
        sq_max = jnp.zeros((1, LANES), F32)
        for c in range(0, sec[1], IN_COL_CHUNK):
            acc = project(sec, c)
            if mul is not None:
                acc = acc * mul
            acc16 = acc.astype(BF16)
            for i in range(IN_COL_CHUNK // LANES):
                pair = c // LANES + i
                d2 = acc16[:, i * LANES:(i + 1) * LANES]
                out_ref[0, 2 * pair] = jnp.where(low, d2, aug_even(pair))
                out_ref[0, 2 * pair + 1] = jnp.where(low, aug_odd(pair), d2)
                if norms:
                    sq = acc[:, i * LANES:(i + 1) * LANES]
                    sq = sq * sq
                    for odd in (0, 1):
                        mine = jnp.logical_not(low) if odd else low
                        rows = jnp.sum(jnp.where(mine, sq, 0.0), axis=-1, keepdims=True)
                        sq_max = jnp.where(head_lane == 2 * pair + odd,
                                           jnp.max(rows, axis=0, keepdims=True), sq_max)
        return sq_max

    pair_cols = lambda a: (lambda pair: a[:, pair * LANES:(pair + 1) * LANES])
    q_sq = head_major(qa_ref, SEC_Q, pair_cols(aq), pair_cols(aq),
                      mul=ATT_HEAD_DIM ** -0.5 * LOG2E, norms=True)
    k_sq = head_major(ka_ref, SEC_K, pair_cols(ak), pair_cols(ak), norms=True)
    head_major(va_ref, SEC_V, lambda pair: one_even, lambda pair: one_odd)
    stats_ref[0] = jnp.concatenate(
        [q_sq, k_sq, c_first, carry * LOG2E, jnp.zeros((SUBLANES - 4, LANES), F32)], axis=0)


def _const_spec(a):
    zeros = (0,) * a.ndim
    return pl.BlockSpec(a.shape, lambda *_: zeros, pipeline_mode=pl.Buffered(1))


def _in_proj(x2, g, w_main, w_tail, fgb, bsz, seqlen):
    n = x2.shape[0]
    nt = seqlen // IN_ROWS
    tri3 = _tri3()
    eq, ek = _aug_matrices()
    row = lambda w: pl.BlockSpec((IN_ROWS, w), lambda b, t: (b * nt + t, 0))
    full = _const_spec
    head_major = pl.BlockSpec((1, ATT_HEADS, IN_ROWS, LANES), lambda b, t: (b, 0, t, 0))
    aug_shape = jax.ShapeDtypeStruct((bsz, ATT_HEADS, seqlen, LANES), BF16)
    out_shapes = (
        jax.ShapeDtypeStruct((n, SSD_WIDTH), BF16),
        jax.ShapeDtypeStruct((n, CONV_CH), BF16),
        jax.ShapeDtypeStruct((n, ATT_WIDTH), BF16),
        aug_shape, aug_shape, aug_shape,
        jax.ShapeDtypeStruct((n, LANES), F32),
        jax.ShapeDtypeStruct((bsz * nt, SUBLANES, LANES), F32),
    )
    return pl.pallas_call(
        _in_proj_kernel,
        out_shape=out_shapes,
        grid=(bsz, nt),
        in_specs=[row(D_MODEL), full(g), full(w_main), full(w_tail), full(fgb),
                  full(tri3), full(eq), full(ek)],
        out_specs=(row(SSD_WIDTH), row(CONV_CH), row(ATT_WIDTH),
                   head_major, head_major, head_major, row(LANES),
                   pl.BlockSpec((1, SUBLANES, LANES), lambda b, t: (b * nt + t, 0, 0))),
        scratch_shapes=[pltpu.VMEM((1, LANES), F32)],
        compiler_params=pltpu.CompilerParams(
            dimension_semantics=("arbitrary", "arbitrary"), vmem_limit_bytes=VMEM_LIMIT),
        name="in_proj",
    )(x2, g, w_main, w_tail, fgb, tri3, eq, ek)


def _head_expand2():
    e = np.zeros((2 * LANES, SSD_WIDTH), np.float32)
    for h in range(SSD_HEADS):
        e[TAIL_DT + h, h * SSD_HEAD_DIM:(h + 1) * SSD_HEAD_DIM] = 1.0
        e[LANES + TAIL_DT + h, h * SSD_HEAD_DIM:(h + 1) * SSD_HEAD_DIM] = 1.0
    return jnp.asarray(e, BF16)


def _conv_shift():
    s = np.zeros((CONV_WIDTH * CHUNK, CONV_HALO + CHUNK), np.float32)
    for j in range(CONV_WIDTH):
        for l in range(CHUNK):
            s[j * CHUNK + l, CONV_HALO + l - (CONV_WIDTH - 1) + j] = 1.0
    return jnp.asarray(s, BF16)


def _ssd_kernel(xbc_ref, tail_ref, z_ref, cw_ref, cb_ref, dtb_ref, a_ref, dfull_ref, g_ref,
                tri3_ref, e2_ref, shift_ref, xprev_ref, y_ref, xwin_ref, state_ref):
    @pl.when(pl.program_id(1) == 0)
    def _():
        state_ref[...] = jnp.zeros_like(state_ref)

    prev = xprev_ref[...]
    xwin_ref[0:CONV_HALO, :] = jnp.where(pl.program_id(1) > 0, prev, jnp.zeros_like(prev))
    xwin_ref[CONV_HALO:CONV_HALO + SSD_ROWS, :] = xbc_ref[...]

    tri3 = tri3_ref[...]
    row_i = lax.broadcasted_iota(jnp.int32, (CHUNK, CHUNK), 0)
    col_i = lax.broadcasted_iota(jnp.int32, (CHUNK, CHUNK), 1)
    causal = row_i >= col_i
    low = col_i < HALF

    for c in range(SSD_ROWS // CHUNK):
        r0 = c * CHUNK
        window = xwin_ref[r0:r0 + CONV_HALO + CHUNK, :]
        taps = jnp.dot(shift_ref[...], window, preferred_element_type=F32)
        conv = cb_ref[...]
        for j in range(CONV_WIDTH):
            conv = conv + cw_ref[j:j + 1, :] * taps[j * CHUNK:(j + 1) * CHUNK]
        xc = _silu(conv)
        xs = xc[:, :SSD_WIDTH]
        bm = xc[:, SSD_WIDTH:SSD_WIDTH + N_GROUPS * D_STATE].astype(BF16)
        cm = xc[:, SSD_WIDTH + N_GROUPS * D_STATE:].astype(BF16)

        dt = _softplus(tail_ref[r0:r0 + CHUNK, :] + dtb_ref[...])
        adt = dt * a_ref[...]
        acs = _cumsum_rows(tri3, adt)
        acs_t = acs.T
        last = acs[CHUNK - 1:CHUNK, :]
        w = dt * jnp.exp(last - acs)
        ea = jnp.exp(acs)
        cd = jnp.broadcast_to(jnp.exp(last), (BF16_ROWS, LANES))
        stack = jnp.concatenate([dt, w, ea, cd], axis=0)
        lhs = jnp.concatenate(_split_bf16(stack, 2), axis=1)
        spread = jnp.dot(lhs, e2_ref[...], preferred_element_type=F32)
        dt_full = spread[0:CHUNK]
        w_full = spread[CHUNK:2 * CHUNK]
        ea_full = spread[2 * CHUNK:3 * CHUNK]
        cd_full = spread[3 * CHUNK:3 * CHUNK + 1]

        xdt = (xs * dt_full).astype(BF16)
        wx = (xs * w_full).astype(BF16)

        y_blocks = []
        yoff_blocks = []
        for g in range(N_GROUPS):
            gs = slice(g * D_STATE, (g + 1) * D_STATE)
            cb_g = lax.dot_general(cm[:, gs], bm[:, gs], (((1,), (1,)), ((), ())),
                                   preferred_element_type=F32)
            for jp in range(HEADS_PER_GROUP // 2):
                pair = g * (HEADS_PER_GROUP // 2) + jp
                x_pair = xdt[:, pair * LANES:(pair + 1) * LANES]
                ys = []
                for r in (2 * pair, 2 * pair + 1):
                    seg = acs[:, r:r + 1] - acs_t[r:r + 1, :]
                    dec = jnp.exp(jnp.where(causal, seg, -jnp.inf))
                    m = (cb_g * dec).astype(BF16)
                    ys.append(jnp.dot(m, x_pair, preferred_element_type=F32))
                y_blocks.append(jnp.where(low, ys[0], ys[1]))
            hs = slice(g * GROUP_WIDTH, (g + 1) * GROUP_WIDTH)
            st = state_ref[:, hs]
            yoff_blocks.append(jnp.dot(cm[:, gs], st.astype(BF16), preferred_element_type=F32))
            s_g = lax.dot_general(bm[:, gs], wx[:, hs], (((0,), (0,)), ((), ())),
                                  preferred_element_type=F32)
            state_ref[:, hs] = st * cd_full[:, hs] + s_g
        y = jnp.concatenate(y_blocks, axis=1)
        y_off = jnp.concatenate(yoff_blocks, axis=1)
        y = y + y_off * ea_full + dfull_ref[...] * xs
        y = y * _silu(z_ref[r0:r0 + CHUNK, :].astype(F32))
        outs = []
        for g in range(N_GROUPS):
            yg = y[:, g * GROUP_WIDTH:(g + 1) * GROUP_WIDTH]
            ms = jnp.mean(yg * yg, axis=-1, keepdims=True)
            outs.append(yg * lax.rsqrt(ms + EPS))
        yn = jnp.concatenate(outs, axis=1) * g_ref[...]
        y_ref[r0:r0 + CHUNK, :] = yn.astype(y_ref.dtype)


def _ssd(xbc, tail, z_ssd, conv_w, conv_b, dtb, a_neg, d_full, g, bsz, seqlen):
    nt = seqlen // SSD_ROWS
    tri3 = _tri3()
    e2 = _head_expand2()
    shift = _conv_shift()
    row = lambda w: pl.BlockSpec((SSD_ROWS, w), lambda b, t: (b * nt + t, 0))
    full = _const_spec
    halo_per_tile = SSD_ROWS // CONV_HALO
    prev = pl.BlockSpec(
        (CONV_HALO, CONV_CH),
        lambda b, t: (jnp.maximum((b * nt + t) * halo_per_tile - 1, 0), 0))
    return pl.pallas_call(
        _ssd_kernel,
        out_shape=jax.ShapeDtypeStruct((bsz * seqlen, SSD_WIDTH), BF16),
        grid=(bsz, nt),
        in_specs=[row(CONV_CH), row(LANES), row(SSD_WIDTH), full(conv_w), full(conv_b),
                  full(dtb), full(a_neg), full(d_full), full(g), full(tri3), full(e2),
                  full(shift), prev],
        out_specs=row(SSD_WIDTH),
        scratch_shapes=[pltpu.VMEM((CONV_HALO + SSD_ROWS, CONV_CH), BF16),
                        pltpu.VMEM((D_STATE, SSD_WIDTH), F32)],
        compiler_params=pltpu.CompilerParams(
            dimension_semantics=("arbitrary", "arbitrary"), vmem_limit_bytes=VMEM_LIMIT),
        name="ssd",
    )(xbc, tail, z_ssd, conv_w, conv_b, dtb, a_neg, d_full, g, tri3, e2, shift, xbc)


def _fox_attn_kernel(skip_ref, qa_ref, ka_ref, va_ref, z_ref, g_ref, o_ref, s_ref, acc_ref,
                     m_ref):
    qi = pl.program_id(2)
    lane = lax.broadcasted_iota(jnp.int32, (ATT_TQ, LANES), 1)
    low = lane < HALF
    nt_dims = (((1,), (1,)), ((), ()))
    n_diag = ATT_TQ // ATT_TK
    all_rows = (0, ATT_TQ)

    def scores(hh, kstart, rows=all_rows):
        k = ka_ref[0, hh, pl.ds(kstart, ATT_TK), :]
        s_ref[hh, rows[0]:rows[1], :] = lax.dot_general(
            qa_ref[0, hh, rows[0]:rows[1], :], k, nt_dims, preferred_element_type=F32)

    def softmax_pv(hh, kstart, rows=all_rows, diag=None):
        r0, r1 = rows
        s = s_ref[hh, r0:r1, :]
        if diag is not None:
            row_i = lax.broadcasted_iota(jnp.int32, (r1 - r0, ATT_TK), 0) + r0
            col_i = lax.broadcasted_iota(jnp.int32, (r1 - r0, ATT_TK), 1) + diag * ATT_TK
            s = jnp.where(row_i >= col_i, s, -jnp.inf)
        m_old = m_ref[hh, r0:r1, :]
        m_new = jnp.maximum(m_old, jnp.max(s, axis=-1, keepdims=True))
        p = jnp.exp2(s - jnp.tile(m_new, (1, ATT_TK // LANES))).astype(BF16)
        alpha = jnp.exp2(m_old - m_new)
        m_ref[hh, r0:r1, :] = m_new
        v = va_ref[0, hh, pl.ds(kstart, ATT_TK), :]
        acc_ref[hh, r0:r1, :] = (alpha * acc_ref[hh, r0:r1, :]
                                 + jnp.dot(p, v, preferred_element_type=F32))

    first = skip_ref[(pl.program_id(0) * pl.num_programs(1) + pl.program_id(1))
                     * pl.num_programs(2) + qi]
    m_ref[...] = jnp.full(m_ref.shape, NEG_BIG, F32)
    acc_ref[...] = jnp.zeros(acc_ref.shape, F32)
    scores(0, pl.multiple_of(first * ATT_TQ, ATT_TQ))

    def body(j, carry):
        for u in range(n_diag):
            kstart = pl.multiple_of(j * ATT_TQ, ATT_TQ) + u * ATT_TK
            scores(1, kstart)
            softmax_pv(0, kstart)
            scores(0, kstart + ATT_TK)
            softmax_pv(1, kstart)
        return carry

    lax.fori_loop(first, qi, body, 0)
    kdiag = pl.multiple_of(qi * ATT_TQ, ATT_TQ)
    for d in range(n_diag):
        rows = (d * ATT_TK, ATT_TQ)
        scores(1, kdiag + d * ATT_TK, rows)
        softmax_pv(0, kdiag + d * ATT_TK, rows, diag=d)
        if d + 1 < n_diag:
            scores(0, kdiag + (d + 1) * ATT_TK, ((d + 1) * ATT_TK, ATT_TQ))
        softmax_pv(1, kdiag + d * ATT_TK, rows, diag=d)

    def normed(hh):
        acc = acc_ref[hh]
        l_lane = 0 if hh else HALF
        mine = jnp.logical_not(low) if hh else low
        l = jnp.sum(jnp.where(lane == l_lane, acc, 0.0), axis=-1, keepdims=True)
        ssq = jnp.sum(jnp.where(mine, acc * acc, 0.0), axis=-1, keepdims=True)
        inv_l = 1.0 / l
        return acc * (inv_l * lax.rsqrt(ssq * (inv_l * inv_l) * (1.0 / ATT_HEAD_DIM) + EPS))

    att = jnp.where(low, normed(0), normed(1)) * g_ref[...]
    o_ref[...] = (att * _silu(z_ref[...].astype(F32))).astype(o_ref.dtype)


def _skip_table(stats, bsz, seqlen):
    assert ATT_TK == IN_ROWS and ATT_TQ % ATT_TK == 0
    nt = seqlen // ATT_TK
    nq = seqlen // ATT_TQ
    per_q = ATT_TQ // ATT_TK
    st = stats.reshape(bsz, nt, SUBLANES, LANES)
    qn = jnp.sqrt(st[:, :, 0, :ATT_HEADS]) * NORM_SLACK
    kn = jnp.sqrt(st[:, :, 1, :ATT_HEADS]) * NORM_SLACK
    c_first = st[:, :, 2, TAIL_F:TAIL_F + ATT_HEADS]
    c_last = st[:, :, 3, TAIL_F:TAIL_F + ATT_HEADS]
    by_q = lambda a: a.reshape(bsz, nq, per_q, ATT_HEADS)
    qmax = by_q(qn).max(axis=2)[:, :, None, :]
    kdiag = by_q(kn).max(axis=2)[:, :, None, :]
    decay = c_last[:, None, :, :] - by_q(c_first)[:, :, 0, None, :]
    key_tile = jnp.arange(nt, dtype=jnp.int32)[None, None, :, None]
    before = key_tile < (jnp.arange(nq, dtype=jnp.int32) * per_q)[None, :, None, None]
    dead = (qmax * (kn[:, None, :, :] + kdiag) + SKIP_MARGIN < decay) & before
    lead = jnp.min(jnp.where(dead, nt, key_tile), axis=2)
    lead = lead.reshape(bsz, nq, ATT_HEADS // 2, 2).min(axis=-1)
    first_trip = lead // per_q
    return jnp.transpose(first_trip, (0, 2, 1)).reshape(-1).astype(jnp.int32)


def _fox_attn(qa, ka, va, z_att, g2, skip):
    bsz, _, seqlen, _ = qa.shape
    nq = seqlen // ATT_TQ
    q_spec = pl.BlockSpec((1, 2, ATT_TQ, LANES), lambda b, p, i, s: (b, p, i, 0))
    kv_spec = pl.BlockSpec((1, 2, seqlen, LANES), lambda b, p, i, s: (b, p, 0, 0))
    row_spec = pl.BlockSpec((ATT_TQ, LANES), lambda b, p, i, s: (b * nq + i, p))
    return pl.pallas_call(
        _fox_attn_kernel,
        out_shape=jax.ShapeDtypeStruct((bsz * seqlen, ATT_WIDTH), BF16),
        grid_spec=pltpu.PrefetchScalarGridSpec(
            num_scalar_prefetch=1,
            grid=(bsz, ATT_HEADS // 2, nq),
            in_specs=[q_spec, kv_spec, kv_spec, row_spec,
                      pl.BlockSpec(g2.shape, lambda b, p, i, s: (0, 0))],
            out_specs=row_spec,
            scratch_shapes=[pltpu.VMEM((2, ATT_TQ, ATT_TK), F32),
                            pltpu.VMEM((2, ATT_TQ, LANES), F32),
                            pltpu.VMEM((2, ATT_TQ, LANES), F32)]),
        compiler_params=pltpu.CompilerParams(
            dimension_semantics=("arbitrary", "arbitrary", "arbitrary"),
            vmem_limit_bytes=VMEM_LIMIT),
        name="fox_attn",
    )(skip, qa, ka, va, z_att, g2)


def _rms(h, g):
    ms = jnp.mean(h * h, axis=-1, keepdims=True)
    return h * lax.rsqrt(ms + EPS) * g


def _out_ple_kernel(x_ref, ys_ref, ya_ref, p_ref, wo_s_ref, wo_a_ref, wg_ref, wp_ref,
                    gp_ref, gf_ref, o_ref):
    h = x_ref[...]
    h = h + (jnp.dot(ys_ref[...], wo_s_ref[...], preferred_element_type=F32)
             + jnp.dot(ya_ref[...], wo_a_ref[...], preferred_element_type=F32))
    hn = _rms(h, gp_ref[...]).astype(BF16)
    gate = jax.nn.sigmoid(jnp.dot(hn, wg_ref[...], preferred_element_type=F32))
    emb = jnp.dot(p_ref[...].astype(BF16), wp_ref[...], preferred_element_type=F32)
    h = h + gate * emb
    o_ref[...] = _rms(h, gf_ref[...])


def _out_ple(x2, y_ssd, y_att, p2, wo_s, wo_a, wg, wp, gp, gf):
    n = x2.shape[0]
    row = lambda w: pl.BlockSpec((OUT_ROWS, w), lambda i: (i, 0))
    full = _const_spec
    return pl.pallas_call(
        _out_ple_kernel,
        out_shape=jax.ShapeDtypeStruct((n, D_MODEL), F32),
        grid=(n // OUT_ROWS,),
        in_specs=[row(D_MODEL), row(SSD_WIDTH), row(ATT_WIDTH), row(PLE_DIM),
                  full(wo_s), full(wo_a), full(wg), full(wp), full(gp), full(gf)],
        out_specs=row(D_MODEL),
        compiler_params=pltpu.CompilerParams(
            dimension_semantics=("arbitrary",), vmem_limit_bytes=VMEM_LIMIT),
        name="out_ple",
    )(x2, y_ssd, y_att, p2, wo_s, wo_a, wg, wp, gp, gf)


def _layer(h2, p2, bsz, seqlen, norm_g, w_in, conv_w, conv_b, dt_bias, a_log, d_skip,
           ssd_norm_g, fg_bias, att_norm_g, w_out, ple_norm_g, w_ple_gate, w_ple_proj, out_g):
    o_zs = 0
    o_xbc = o_zs + SSD_WIDTH
    o_dt = o_xbc + CONV_CH
    o_za = o_dt + SSD_HEADS
    o_q = o_za + ATT_WIDTH
    o_k = o_q + ATT_WIDTH
    o_v = o_k + ATT_WIDTH
    o_f = o_v + ATT_WIDTH
    w_in16 = w_in.astype(BF16)
    order = jnp.argsort(fg_bias)
    fg_bias = fg_bias[order]

    def heads_cols(w):
        return w.reshape(D_MODEL, ATT_HEADS, ATT_HEAD_DIM)[:, order].reshape(D_MODEL, ATT_WIDTH)

    att_cols = [heads_cols(w_in16[:, o:o + ATT_WIDTH]) for o in (o_za, o_q, o_k, o_v)]
    w_main = jnp.concatenate([w_in16[:, o_zs:o_dt]] + att_cols, axis=1)
    w_f = w_in16[:, o_f:o_f + ATT_HEADS][:, order]
    w_tail = jnp.concatenate(
        [w_in16[:, o_dt:o_za]] + [w_f] * SPLIT_PARTS
        + [jnp.zeros((D_MODEL, LANES - SSD_HEADS - SPLIT_PARTS * ATT_HEADS), BF16)], axis=1)
    w_out_att = w_out[SSD_WIDTH:].astype(BF16).reshape(
        ATT_HEADS, ATT_HEAD_DIM, D_MODEL)[order].reshape(ATT_WIDTH, D_MODEL)

    def lanes16(v, off):
        return jnp.zeros((1, LANES), F32).at[0, off:off + v.shape[0]].set(v.astype(F32))

    z_ssd, xbc, z_att, qa, ka, va, tail, stats = _in_proj(
        h2, norm_g.reshape(1, D_MODEL).astype(F32), w_main, w_tail,
        lanes16(jnp.tile(fg_bias, SPLIT_PARTS), TAIL_F), bsz, seqlen)

    y_ssd = _ssd(
        xbc, tail, z_ssd, conv_w.astype(F32), conv_b.reshape(1, CONV_CH).astype(F32),
        lanes16(dt_bias, TAIL_DT), lanes16(-jnp.exp(a_log.astype(F32)), TAIL_DT),
        jnp.repeat(d_skip.astype(F32), SSD_HEAD_DIM).reshape(1, SSD_WIDTH),
        ssd_norm_g.reshape(1, SSD_WIDTH).astype(F32), bsz, seqlen)

    y_att = _fox_attn(qa, ka, va, z_att,
                      jnp.tile(att_norm_g.astype(F32), 2).reshape(1, LANES),
                      _skip_table(stats, bsz, seqlen))

    return _out_ple(
        h2, y_ssd, y_att, p2,
        w_out[:SSD_WIDTH].astype(BF16), w_out_att,
        w_ple_gate.astype(BF16), w_ple_proj.astype(BF16),
        ple_norm_g.reshape(1, D_MODEL).astype(F32), out_g.reshape(1, D_MODEL).astype(F32))


def kernel(x, p, norm_g, w_in, conv_w, conv_b, dt_bias, a_log, d_skip, ssd_norm_g, fg_bias,
           att_norm_g, w_out, ple_norm_g, w_ple_gate, w_ple_proj, final_norm_g):
    bsz, seqlen, _ = x.shape
    depth = p.shape[0]
    assert depth == 1, "the fused tail applies the final norm right after the only layer"
    h2 = x.reshape(bsz * seqlen, D_MODEL)
    out = _layer(h2, p[0].reshape(bsz * seqlen, PLE_DIM), bsz, seqlen,
                 norm_g[0], w_in[0], conv_w[0], conv_b[0], dt_bias[0], a_log[0], d_skip[0],
                 ssd_norm_g[0], fg_bias[0], att_norm_g[0], w_out[0], ple_norm_g[0],
                 w_ple_gate[0], w_ple_proj[0], final_norm_g)
    return out.reshape(bsz, seqlen, D_MODEL)
```

```python
import functools
import math

import jax
import jax.numpy as jnp
import numpy as np
from jax import lax
from jax.experimental import pallas as pl
from jax.experimental.pallas import tpu as pltpu

F32 = jnp.float32
BF16 = jnp.bfloat16

D_MODEL = 1024
SSD_WIDTH = 1024
ATT_WIDTH = 1024
SSD_HEAD_DIM = 64
SSD_HEADS = 16
N_GROUPS = 2
HEADS_PER_GROUP = SSD_HEADS // N_GROUPS
GROUP_WIDTH = SSD_WIDTH // N_GROUPS
D_STATE = 128
CONV_WIDTH = 4
CHUNK = 128
ATT_HEAD_DIM = 64
ATT_HEADS = 16
PLE_DIM = 256
EPS = 1e-6
CONV_CH = SSD_WIDTH + 2 * N_GROUPS * D_STATE

LANES = 128
SUBLANES = 8
HALF = LANES // 2
BF16_ROWS = 16
CONV_HALO = 128

LOG2E = 1.4426950408889634
NEG_BIG = -1e30
SKIP_MARGIN = 136.0
NORM_SLACK = 1.01

SEC_ZS = (0, SSD_WIDTH)
SEC_XBC = (SEC_ZS[0] + SEC_ZS[1], CONV_CH)
SEC_ZA = (SEC_XBC[0] + SEC_XBC[1], ATT_WIDTH)
SEC_Q = (SEC_ZA[0] + SEC_ZA[1], ATT_WIDTH)
SEC_K = (SEC_Q[0] + SEC_Q[1], ATT_WIDTH)
SEC_V = (SEC_K[0] + SEC_K[1], ATT_WIDTH)
MAIN_COLS = SEC_V[0] + SEC_V[1]
TAIL_DT = 0
TAIL_F = SSD_HEADS
SPLIT_PARTS = 3
TAIL_ONE = TAIL_DT

IN_ROWS = 512
IN_COL_CHUNK = 512
SSD_ROWS = 256
ATT_TQ = 1024
ATT_TK = 512
OUT_ROWS = 1024
VMEM_LIMIT = 56 * 1024 * 1024


def _split_bf16(x, parts):
    out = []
    r = x
    for i in range(parts):
        h = r.astype(BF16)
        out.append(h)
        if i + 1 < parts:
            r = r - h.astype(F32)
    return out


def _softplus(x):
    return jnp.maximum(x, 0.0) + jnp.log1p(jnp.exp(-jnp.abs(x)))


def _silu(x):
    return x * jax.nn.sigmoid(x)


def _aug_lane(head, j):
    pair, odd = divmod(head, 2)
    return pair * LANES + (j if odd else HALF + j)


def _aug_matrices():
    eq = np.zeros((LANES, ATT_WIDTH), np.float32)
    ek = np.zeros((LANES, ATT_WIDTH), np.float32)
    for h in range(ATT_HEADS):
        for part in range(SPLIT_PARTS):
            src = TAIL_F + part * ATT_HEADS + h
            eq[src, _aug_lane(h, part)] = 1.0
            ek[src, _aug_lane(h, SPLIT_PARTS + part)] = -1.0
            eq[TAIL_ONE, _aug_lane(h, SPLIT_PARTS + part)] = 1.0
            ek[TAIL_ONE, _aug_lane(h, part)] = 1.0
    return jnp.asarray(eq, BF16), jnp.asarray(ek, BF16)


def _tri3():
    tri = np.tril(np.ones((CHUNK, CHUNK), np.float32))
    return jnp.asarray(np.concatenate([tri, tri, tri], axis=1), BF16)


def _cumsum_rows(tri3, v):
    parts = _split_bf16(v, 3)
    return jnp.dot(tri3, jnp.concatenate(parts, axis=0), preferred_element_type=F32)


def _in_proj_kernel(x_ref, g_ref, wm_ref, wt_ref, fgb_ref, tri3_ref, eq_ref, ek_ref,
                    zs_ref, xbc_ref, za_ref, qa_ref, ka_ref, va_ref, tail_ref, stats_ref,
                    carry_ref):
    @pl.when(pl.program_id(1) == 0)
    def _():
        carry_ref[...] = jnp.zeros_like(carry_ref)

    x = x_ref[...]
    ms = jnp.mean(x * x, axis=-1, keepdims=True)
    u = (x * lax.rsqrt(ms + EPS) * g_ref[...]).astype(BF16)

    def project(sec, c):
        return jnp.dot(u, wm_ref[:, sec[0] + c:sec[0] + c + IN_COL_CHUNK],
                       preferred_element_type=F32)

    def section(out_ref, sec):
        for c in range(0, sec[1], IN_COL_CHUNK):
            out_ref[:, c:c + IN_COL_CHUNK] = project(sec, c).astype(out_ref.dtype)

    tail = jnp.dot(u, wt_ref[...], preferred_element_type=F32)
    tail_ref[...] = tail

    tri3 = tri3_ref[...]
    part_lane = lax.broadcasted_iota(jnp.int32, (CHUNK, LANES), 1)
    carry = carry_ref[...]
    lhs = []
    for c in range(IN_ROWS // CHUNK):
        log_f = -_softplus(-(tail[c * CHUNK:(c + 1) * CHUNK] + fgb_ref[...]))
        cum = _cumsum_rows(tri3, log_f) + carry
        carry = cum[CHUNK - 1:CHUNK, :]
        if c == 0:
            c_first = cum[0:1, :] * LOG2E
        hi, mid, lo = _split_bf16(cum * LOG2E, SPLIT_PARTS)
        parts = jnp.where(part_lane < TAIL_F + ATT_HEADS, hi,
                          jnp.where(part_lane < TAIL_F + 2 * ATT_HEADS, mid, lo))
        lhs.append(jnp.where(part_lane == TAIL_ONE, jnp.ones_like(parts), parts))
    carry_ref[...] = carry
    lhs = jnp.concatenate(lhs, axis=0)
    aq = jnp.dot(lhs, eq_ref[...], preferred_element_type=F32).astype(BF16)
    ak = jnp.dot(lhs, ek_ref[...], preferred_element_type=F32).astype(BF16)

    section(zs_ref, SEC_ZS)
    section(xbc_ref, SEC_XBC)
    section(za_ref, SEC_ZA)

    lane = lax.broadcasted_iota(jnp.int32, (IN_ROWS, LANES), 1)
    low = lane < HALF
    one_even = jnp.where(lane == HALF, 1.0, 0.0).astype(BF16)
    one_odd = jnp.where(lane == 0, 1.0, 0.0).astype(BF16)

    head_lane = lax.broadcasted_iota(jnp.int32, (1, LANES), 1)

    def head_major(out_ref, sec, aug_even, aug_odd, mul=None, norms=False):
        sq_max = jnp.zeros((1, LANES), F32)
        for c in range(0, sec[1], IN_COL_CHUNK):
            acc = project(sec, c)
            if mul is not None:
                acc = acc * mul
            acc16 = acc.astype(BF16)
            for i in range(IN_COL_CHUNK // LANES):
                pair = c // LANES + i
                d2 = acc16[:, i * LANES:(i + 1) * LANES]
                out_ref[0, 2 * pair] = jnp.where(low, d2, aug_even(pair))
                out_ref[0, 2 * pair + 1] = jnp.where(low, aug_odd(pair), d2)
                if norms:
                    sq = acc[:, i * LANES:(i + 1) * LANES]
                    sq = sq * sq
                    for odd in (0, 1):
                        mine = jnp.logical_not(low) if odd else low
                        rows = jnp.sum(jnp.where(mine, sq, 0.0), axis=-1, keepdims=True)
                        sq_max = jnp.where(head_lane == 2 * pair + odd,
                                           jnp.max(rows, axis=0, keepdims=True), sq_max)
        return sq_max

    pair_cols = lambda a: (lambda pair: a[:, pair * LANES:(pair + 1) * LANES])
    q_sq = head_major(qa_ref, SEC_Q, pair_cols(aq), pair_cols(aq),
                      mul=ATT_HEAD_DIM ** -0.5 * LOG2E, norms=True)
    k_sq = head_major(ka_ref, SEC_K, pair_cols(ak), pair_cols(ak), norms=True)
    head_major(va_ref, SEC_V, lambda pair: one_even, lambda pair: one_odd)
    stats_ref[0] = jnp.concatenate(
        [q_sq, k_sq, c_first, carry * LOG2E, jnp.zeros((SUBLANES - 4, LANES), F32)], axis=0)


def _const_spec(a):
    zeros = (0,) * a.ndim
    return pl.BlockSpec(a.shape, lambda *_: zeros, pipeline_mode=pl.Buffered(1))


def _in_proj(x2, g, w_main, w_tail, fgb, bsz, seqlen):
    n = x2.shape[0]
    nt = seqlen // IN_ROWS
    tri3 = _tri3()
    eq, ek = _aug_matrices()
    row = lambda w: pl.BlockSpec((IN_ROWS, w), lambda b, t: (b * nt + t, 0))
    full = _const_spec
    head_major = pl.BlockSpec((1, ATT_HEADS, IN_ROWS, LANES), lambda b, t: (b, 0, t, 0))
    aug_shape = jax.ShapeDtypeStruct((bsz, ATT_HEADS, seqlen, LANES), BF16)
    out_shapes = (
        jax.ShapeDtypeStruct((n, SSD_WIDTH), BF16),
        jax.ShapeDtypeStruct((n, CONV_CH), BF16),
        jax.ShapeDtypeStruct((n, ATT_WIDTH), BF16),
        aug_shape, aug_shape, aug_shape,
        jax.ShapeDtypeStruct((n, LANES), F32),
        jax.ShapeDtypeStruct((bsz * nt, SUBLANES, LANES), F32),
    )
    return pl.pallas_call(
        _in_proj_kernel,
        out_shape=out_shapes,
        grid=(bsz, nt),
        in_specs=[row(D_MODEL), full(g), full(w_main), full(w_tail), full(fgb),
                  full(tri3), full(eq), full(ek)],
        out_specs=(row(SSD_WIDTH), row(CONV_CH), row(ATT_WIDTH),
                   head_major, head_major, head_major, row(LANES),
                   pl.BlockSpec((1, SUBLANES, LANES), lambda b, t: (b * nt + t, 0, 0))),
        scratch_shapes=[pltpu.VMEM((1, LANES), F32)],
        compiler_params=pltpu.CompilerParams(
            dimension_semantics=("arbitrary", "arbitrary"), vmem_limit_bytes=VMEM_LIMIT),
        name="in_proj",
    )(x2, g, w_main, w_tail, fgb, tri3, eq, ek)


def _head_expand2():
    e = np.zeros((2 * LANES, SSD_WIDTH), np.float32)
    for h in range(SSD_HEADS):
        e[TAIL_DT + h, h * SSD_HEAD_DIM:(h + 1) * SSD_HEAD_DIM] = 1.0
        e[LANES + TAIL_DT + h, h * SSD_HEAD_DIM:(h + 1) * SSD_HEAD_DIM] = 1.0
    return jnp.asarray(e, BF16)


def _conv_shift():
    s = np.zeros((CONV_WIDTH * CHUNK, CONV_HALO + CHUNK), np.float32)
    for j in range(CONV_WIDTH):
        for l in range(CHUNK):
            s[j * CHUNK + l, CONV_HALO + l - (CONV_WIDTH - 1) + j] = 1.0
    return jnp.asarray(s, BF16)


def _ssd_kernel(xbc_ref, tail_ref, z_ref, cw_ref, cb_ref, dtb_ref, a_ref, dfull_ref, g_ref,
                tri3_ref, e2_ref, shift_ref, xprev_ref, y_ref, xwin_ref, state_ref):
    @pl.when(pl.program_id(1) == 0)
    def _():
        state_ref[...] = jnp.zeros_like(state_ref)

    prev = xprev_ref[...]
    xwin_ref[0:CONV_HALO, :] = jnp.where(pl.program_id(1) > 0, prev, jnp.zeros_like(prev))
    xwin_ref[CONV_HALO:CONV_HALO + SSD_ROWS, :] = xbc_ref[...]

    tri3 = tri3_ref[...]
    row_i = lax.broadcasted_iota(jnp.int32, (CHUNK, CHUNK), 0)
    col_i = lax.broadcasted_iota(jnp.int32, (CHUNK, CHUNK), 1)
    causal = row_i >= col_i
    low = col_i < HALF

    for c in range(SSD_ROWS // CHUNK):
        r0 = c * CHUNK
        window = xwin_ref[r0:r0 + CONV_HALO + CHUNK, :]
        taps = jnp.dot(shift_ref[...], window, preferred_element_type=F32)
        conv = cb_ref[...]
        for j in range(CONV_WIDTH):
            conv = conv + cw_ref[j:j + 1, :] * taps[j * CHUNK:(j + 1) * CHUNK]
        xc = _silu(conv)
        xs = xc[:, :SSD_WIDTH]
        bm = xc[:, SSD_WIDTH:SSD_WIDTH + N_GROUPS * D_STATE].astype(BF16)
        cm = xc[:, SSD_WIDTH + N_GROUPS * D_STATE:].astype(BF16)

        dt = _softplus(tail_ref[r0:r0 + CHUNK, :] + dtb_ref[...])
        adt = dt * a_ref[...]
        acs = _cumsum_rows(tri3, adt)
        acs_t = acs.T
        last = acs[CHUNK - 1:CHUNK, :]
        w = dt * jnp.exp(last - acs)
        ea = jnp.exp(acs)
        cd = jnp.broadcast_to(jnp.exp(last), (BF16_ROWS, LANES))
        stack = jnp.concatenate([dt, w, ea, cd], axis=0)
        lhs = jnp.concatenate(_split_bf16(stack, 2), axis=1)
        spread = jnp.dot(lhs, e2_ref[...], preferred_element_type=F32)
        dt_full = spread[0:CHUNK]
        w_full = spread[CHUNK:2 * CHUNK]
        ea_full = spread[2 * CHUNK:3 * CHUNK]
        cd_full = spread[3 * CHUNK:3 * CHUNK + 1]

        xdt = (xs * dt_full).astype(BF16)
        wx = (xs * w_full).astype(BF16)

        y_blocks = []
        yoff_blocks = []
        for g in range(N_GROUPS):
            gs = slice(g * D_STATE, (g + 1) * D_STATE)
            cb_g = lax.dot_general(cm[:, gs], bm[:, gs], (((1,), (1,)), ((), ())),
                                   preferred_element_type=F32)
            for jp in range(HEADS_PER_GROUP // 2):
                pair = g * (HEADS_PER_GROUP // 2) + jp
                x_pair = xdt[:, pair * LANES:(pair + 1) * LANES]
                ys = []
                for r in (2 * pair, 2 * pair + 1):
                    seg = acs[:, r:r + 1] - acs_t[r:r + 1, :]
                    dec = jnp.exp(jnp.where(causal, seg, -jnp.inf))
                    m = (cb_g * dec).astype(BF16)
                    ys.append(jnp.dot(m, x_pair, preferred_element_type=F32))
                y_blocks.append(jnp.where(low, ys[0], ys[1]))
            hs = slice(g * GROUP_WIDTH, (g + 1) * GROUP_WIDTH)
            st = state_ref[:, hs]
            yoff_blocks.append(jnp.dot(cm[:, gs], st.astype(BF16), preferred_element_type=F32))
            s_g = lax.dot_general(bm[:, gs], wx[:, hs], (((0,), (0,)), ((), ())),
                                  preferred_element_type=F32)
            state_ref[:, hs] = st * cd_full[:, hs] + s_g
        y = jnp.concatenate(y_blocks, axis=1)
        y_off = jnp.concatenate(yoff_blocks, axis=1)
        y = y + y_off * ea_full + dfull_ref[...] * xs
        y = y * _silu(z_ref[r0:r0 + CHUNK, :].astype(F32))
        outs = []
        for g in range(N_GROUPS):
            yg = y[:, g * GROUP_WIDTH:(g + 1) * GROUP_WIDTH]
            ms = jnp.mean(yg * yg, axis=-1, keepdims=True)
            outs.append(yg * lax.rsqrt(ms + EPS))
        yn = jnp.concatenate(outs, axis=1) * g_ref[...]
        y_ref[r0:r0 + CHUNK, :] = yn.astype(y_ref.dtype)


def _ssd(xbc, tail, z_ssd, conv_w, conv_b, dtb, a_neg, d_full, g, bsz, seqlen):
    nt = seqlen // SSD_ROWS
    tri3 = _tri3()
    e2 = _head_expand2()
    shift = _conv_shift()
    row = lambda w: pl.BlockSpec((SSD_ROWS, w), lambda b, t: (b * nt + t, 0))
    full = _const_spec
    halo_per_tile = SSD_ROWS // CONV_HALO
    prev = pl.BlockSpec(
        (CONV_HALO, CONV_CH),
        lambda b, t: (jnp.maximum((b * nt + t) * halo_per_tile - 1, 0), 0))
    return pl.pallas_call(
        _ssd_kernel,
        out_shape=jax.ShapeDtypeStruct((bsz * seqlen, SSD_WIDTH), BF16),
        grid=(bsz, nt),
        in_specs=[row(CONV_CH), row(LANES), row(SSD_WIDTH), full(conv_w), full(conv_b),
                  full(dtb), full(a_neg), full(d_full), full(g), full(tri3), full(e2),
                  full(shift), prev],
        out_specs=row(SSD_WIDTH),
        scratch_shapes=[pltpu.VMEM((CONV_HALO + SSD_ROWS, CONV_CH), BF16),
                        pltpu.VMEM((D_STATE, SSD_WIDTH), F32)],
        compiler_params=pltpu.CompilerParams(
            dimension_semantics=("arbitrary", "arbitrary"), vmem_limit_bytes=VMEM_LIMIT),
        name="ssd",
    )(xbc, tail, z_ssd, conv_w, conv_b, dtb, a_neg, d_full, g, tri3, e2, shift, xbc)


def _fox_attn_kernel(skip_ref, qa_ref, ka_ref, va_ref, z_ref, g_ref, o_ref, s_ref, acc_ref,
                     m_ref):
    qi = pl.program_id(2)
    lane = lax.broadcasted_iota(jnp.int32, (ATT_TQ, LANES), 1)
    low = lane < HALF
    nt_dims = (((1,), (1,)), ((), ()))
    n_diag = ATT_TQ // ATT_TK
    all_rows = (0, ATT_TQ)

    def scores(hh, kstart, rows=all_rows):
        k = ka_ref[0, hh, pl.ds(kstart, ATT_TK), :]
        s_ref[hh, rows[0]:rows[1], :] = lax.dot_general(
            qa_ref[0, hh, rows[0]:rows[1], :], k, nt_dims, preferred_element_type=F32)

    def softmax_pv(hh, kstart, rows=all_rows, diag=None):
        r0, r1 = rows
        s = s_ref[hh, r0:r1, :]
        if diag is not None:
            row_i = lax.broadcasted_iota(jnp.int32, (r1 - r0, ATT_TK), 0) + r0
            col_i = lax.broadcasted_iota(jnp.int32, (r1 - r0, ATT_TK), 1) + diag * ATT_TK
            s = jnp.where(row_i >= col_i, s, -jnp.inf)
        m_old = m_ref[hh, r0:r1, :]
        m_new = jnp.maximum(m_old, jnp.max(s, axis=-1, keepdims=True))
        p = jnp.exp2(s - jnp.tile(m_new, (1, ATT_TK // LANES))).astype(BF16)
        alpha = jnp.exp2(m_old - m_new)
        m_ref[hh, r0:r1, :] = m_new
        v = va_ref[0, hh, pl.ds(kstart, ATT_TK), :]
        acc_ref[hh, r0:r1, :] = (alpha * acc_ref[hh, r0:r1, :]
                                 + jnp.dot(p, v, preferred_element_type=F32))

    first = skip_ref[(pl.program_id(0) * pl.num_programs(1) + pl.program_id(1))
                     * pl.num_programs(2) + qi]
    m_ref[...] = jnp.full(m_ref.shape, NEG_BIG, F32)
    acc_ref[...] = jnp.zeros(acc_ref.shape, F32)
    scores(0, pl.multiple_of(first * ATT_TK, ATT_TK))

    def key_tile_step(kstart):
        scores(1, kstart)
        softmax_pv(0, kstart)
        scores(0, kstart + ATT_TK)
        softmax_pv(1, kstart)

    @pl.when(first % 2 == 1)
    def _():
        key_tile_step(pl.multiple_of(first * ATT_TK, ATT_TK))

    def body(j, carry):
        for u in range(n_diag):
            key_tile_step(pl.multiple_of(j * ATT_TQ, ATT_TQ) + u * ATT_TK)
        return carry

    lax.fori_loop((first + 1) // 2, qi, body, 0)
    kdiag = pl.multiple_of(qi * ATT_TQ, ATT_TQ)
    for d in range(n_diag):
        rows = (d * ATT_TK, ATT_TQ)
        scores(1, kdiag + d * ATT_TK, rows)
        softmax_pv(0, kdiag + d * ATT_TK, rows, diag=d)
        if d + 1 < n_diag:
            scores(0, kdiag + (d + 1) * ATT_TK, ((d + 1) * ATT_TK, ATT_TQ))
        softmax_pv(1, kdiag + d * ATT_TK, rows, diag=d)

    def normed(hh):
        acc = acc_ref[hh]
        l_lane = 0 if hh else HALF
        mine = jnp.logical_not(low) if hh else low
        l = jnp.sum(jnp.where(lane == l_lane, acc, 0.0), axis=-1, keepdims=True)
        ssq = jnp.sum(jnp.where(mine, acc * acc, 0.0), axis=-1, keepdims=True)
        inv_l = 1.0 / l
        return acc * (inv_l * lax.rsqrt(ssq * (inv_l * inv_l) * (1.0 / ATT_HEAD_DIM) + EPS))

    att = jnp.where(low, normed(0), normed(1)) * g_ref[...]
    o_ref[...] = (att * _silu(z_ref[...].astype(F32))).astype(o_ref.dtype)


def _skip_table(stats, bsz, seqlen):
    assert ATT_TK == IN_ROWS and ATT_TQ == 2 * ATT_TK
    nt = seqlen // ATT_TK
    nq = seqlen // ATT_TQ
    per_q = ATT_TQ // ATT_TK
    st = stats.reshape(bsz, nt, SUBLANES, LANES)
    qn = jnp.sqrt(st[:, :, 0, :ATT_HEADS]) * NORM_SLACK
    kn = jnp.sqrt(st[:, :, 1, :ATT_HEADS]) * NORM_SLACK
    c_first = st[:, :, 2, TAIL_F:TAIL_F + ATT_HEADS]
    c_last = st[:, :, 3, TAIL_F:TAIL_F + ATT_HEADS]
    by_q = lambda a: a.reshape(bsz, nq, per_q, ATT_HEADS)
    qmax = by_q(qn).max(axis=2)[:, :, None, :]
    kdiag = by_q(kn).max(axis=2)[:, :, None, :]
    decay = c_last[:, None, :, :] - by_q(c_first)[:, :, 0, None, :]
    key_tile = jnp.arange(nt, dtype=jnp.int32)[None, None, :, None]
    before = key_tile < (jnp.arange(nq, dtype=jnp.int32) * per_q)[None, :, None, None]
    dead = (qmax * (kn[:, None, :, :] + kdiag) + SKIP_MARGIN < decay) & before
    lead = jnp.min(jnp.where(dead, nt, key_tile), axis=2)
    lead = lead.reshape(bsz, nq, ATT_HEADS // 2, 2).min(axis=-1)
    return jnp.transpose(lead, (0, 2, 1)).reshape(-1).astype(jnp.int32)


def _fox_attn(qa, ka, va, z_att, g2, skip):
    bsz, _, seqlen, _ = qa.shape
    nq = seqlen // ATT_TQ
    q_spec = pl.BlockSpec((1, 2, ATT_TQ, LANES), lambda b, p, i, s: (b, p, i, 0))
    kv_spec = pl.BlockSpec((1, 2, seqlen, LANES), lambda b, p, i, s: (b, p, 0, 0))
    row_spec = pl.BlockSpec((ATT_TQ, LANES), lambda b, p, i, s: (b * nq + i, p))
    return pl.pallas_call(
        _fox_attn_kernel,
        out_shape=jax.ShapeDtypeStruct((bsz * seqlen, ATT_WIDTH), BF16),
        grid_spec=pltpu.PrefetchScalarGridSpec(
            num_scalar_prefetch=1,
            grid=(bsz, ATT_HEADS // 2, nq),
            in_specs=[q_spec, kv_spec, kv_spec, row_spec,
                      pl.BlockSpec(g2.shape, lambda b, p, i, s: (0, 0))],
            out_specs=row_spec,
            scratch_shapes=[pltpu.VMEM((2, ATT_TQ, ATT_TK), F32),
                            pltpu.VMEM((2, ATT_TQ, LANES), F32),
                            pltpu.VMEM((2, ATT_TQ, LANES), F32)]),
        compiler_params=pltpu.CompilerParams(
            dimension_semantics=("arbitrary", "arbitrary", "arbitrary"),
            vmem_limit_bytes=VMEM_LIMIT),
        name="fox_attn",
    )(skip, qa, ka, va, z_att, g2)


def _rms(h, g):
    ms = jnp.mean(h * h, axis=-1, keepdims=True)
    return h * lax.rsqrt(ms + EPS) * g


def _out_ple_kernel(x_ref, ys_ref, ya_ref, p_ref, wo_s_ref, wo_a_ref, wg_ref, wp_ref,
                    gp_ref, gf_ref, o_ref):
    h = x_ref[...]
    h = h + (jnp.dot(ys_ref[...], wo_s_ref[...], preferred_element_type=F32)
             + jnp.dot(ya_ref[...], wo_a_ref[...], preferred_element_type=F32))
    hn = _rms(h, gp_ref[...]).astype(BF16)
    gate = jax.nn.sigmoid(jnp.dot(hn, wg_ref[...], preferred_element_type=F32))
    emb = jnp.dot(p_ref[...].astype(BF16), wp_ref[...], preferred_element_type=F32)
    h = h + gate * emb
    o_ref[...] = _rms(h, gf_ref[...])


def _out_ple(x2, y_ssd, y_att, p2, wo_s, wo_a, wg, wp, gp, gf):
    n = x2.shape[0]
    row = lambda w: pl.BlockSpec((OUT_ROWS, w), lambda i: (i, 0))
    full = _const_spec
    return pl.pallas_call(
        _out_ple_kernel,
        out_shape=jax.ShapeDtypeStruct((n, D_MODEL), F32),
        grid=(n // OUT_ROWS,),
        in_specs=[row(D_MODEL), row(SSD_WIDTH), row(ATT_WIDTH), row(PLE_DIM),
                  full(wo_s), full(wo_a), full(wg), full(wp), full(gp), full(gf)],
        out_specs=row(D_MODEL),
        compiler_params=pltpu.CompilerParams(
            dimension_semantics=("arbitrary",), vmem_limit_bytes=VMEM_LIMIT),
        name="out_ple",
    )(x2, y_ssd, y_att, p2, wo_s, wo_a, wg, wp, gp, gf)


def _layer(h2, p2, bsz, seqlen, norm_g, w_in, conv_w, conv_b, dt_bias, a_log, d_skip,
           ssd_norm_g, fg_bias, att_norm_g, w_out, ple_norm_g, w_ple_gate, w_ple_proj, out_g):
    o_zs = 0
    o_xbc = o_zs + SSD_WIDTH
    o_dt = o_xbc + CONV_CH
    o_za = o_dt + SSD_HEADS
    o_q = o_za + ATT_WIDTH
    o_k = o_q + ATT_WIDTH
    o_v = o_k + ATT_WIDTH
    o_f = o_v + ATT_WIDTH
    w_in16 = w_in.astype(BF16)
    order = jnp.argsort(fg_bias)
    fg_bias = fg_bias[order]

    n_att = (o_f - o_za) // ATT_WIDTH
    att_cols = w_in16[:, o_za:o_f].reshape(D_MODEL, n_att, ATT_HEADS, ATT_HEAD_DIM)[:, :, order]
    w_main = jnp.concatenate(
        [w_in16[:, o_zs:o_dt], att_cols.reshape(D_MODEL, n_att * ATT_WIDTH)], axis=1)
    w_f = w_in16[:, o_f:o_f + ATT_HEADS][:, order]
    w_tail = jnp.concatenate(
        [w_in16[:, o_dt:o_za]] + [w_f] * SPLIT_PARTS
        + [jnp.zeros((D_MODEL, LANES - SSD_HEADS - SPLIT_PARTS * ATT_HEADS), BF16)], axis=1)
    w_out_att = w_out[SSD_WIDTH:].astype(BF16).reshape(
        ATT_HEADS, ATT_HEAD_DIM, D_MODEL)[order].reshape(ATT_WIDTH, D_MODEL)

    def lanes16(v, off):
        return jnp.zeros((1, LANES), F32).at[0, off:off + v.shape[0]].set(v.astype(F32))

    z_ssd, xbc, z_att, qa, ka, va, tail, stats = _in_proj(
        h2, norm_g.reshape(1, D_MODEL).astype(F32), w_main, w_tail,
        lanes16(jnp.tile(fg_bias, SPLIT_PARTS), TAIL_F), bsz, seqlen)

    y_ssd = _ssd(
        xbc, tail, z_ssd, conv_w.astype(F32), conv_b.reshape(1, CONV_CH).astype(F32),
        lanes16(dt_bias, TAIL_DT), lanes16(-jnp.exp(a_log.astype(F32)), TAIL_DT),
        jnp.repeat(d_skip.astype(F32), SSD_HEAD_DIM).reshape(1, SSD_WIDTH),
        ssd_norm_g.reshape(1, SSD_WIDTH).astype(F32), bsz, seqlen)

    y_att = _fox_attn(qa, ka, va, z_att,
                      jnp.tile(att_norm_g.astype(F32), 2).reshape(1, LANES),
                      _skip_table(stats, bsz, seqlen))

    return _out_ple(
        h2, y_ssd, y_att, p2,
        w_out[:SSD_WIDTH].astype(BF16), w_out_att,
        w_ple_gate.astype(BF16), w_ple_proj.astype(BF16),
        ple_norm_g.reshape(1, D_MODEL).astype(F32), out_g.reshape(1, D_MODEL).astype(F32))


def kernel(x, p, norm_g, w_in, conv_w, conv_b, dt_bias, a_log, d_skip, ssd_norm_g, fg_bias,
           att_norm_g, w_out, ple_norm_g, w_ple_gate, w_ple_proj, final_norm_g):
    bsz, seqlen, _ = x.shape
    depth = p.shape[0]
    assert depth == 1, "the fused tail applies the final norm right after the only layer"
    h2 = x.reshape(bsz * seqlen, D_MODEL)
    out = _layer(h2, p.reshape(bsz * seqlen, PLE_DIM), bsz, seqlen,
                 norm_g[0], w_in[0], conv_w[0], conv_b[0], dt_bias[0], a_log[0], d_skip[0],
                 ssd_norm_g[0], fg_bias[0], att_norm_g[0], w_out[0], ple_norm_g[0],
                 w_ple_gate[0], w_ple_proj[0], final_norm_g)
    return out.reshape(bsz, seqlen, D_MODEL)
```

```python
import functools
import math

import jax
import jax.numpy as jnp
import numpy as np
from jax import lax
from jax.experimental import pallas as pl
from jax.experimental.pallas import tpu as pltpu

F32 = jnp.float32
BF16 = jnp.bfloat16

D_MODEL = 1024
SSD_WIDTH = 1024
ATT_WIDTH = 1024
SSD_HEAD_DIM = 64
SSD_HEADS = 16
N_GROUPS = 2
HEADS_PER_GROUP = SSD_HEADS // N_GROUPS
GROUP_WIDTH = SSD_WIDTH // N_GROUPS
D_STATE = 128
CONV_WIDTH = 4
CHUNK = 128
ATT_HEAD_DIM = 64
ATT_HEADS = 16
PLE_DIM = 256
EPS = 1e-6
CONV_CH = SSD_WIDTH + 2 * N_GROUPS * D_STATE

LANES = 128
SUBLANES = 8
HALF = LANES // 2
BF16_ROWS = 16
CONV_HALO = 128

LOG2E = 1.4426950408889634
NEG_BIG = -1e30
SKIP_MARGIN = 136.0
NORM_SLACK = 1.01

SEC_ZS = (0, SSD_WIDTH)
SEC_XBC = (SEC_ZS[0] + SEC_ZS[1], CONV_CH)
SEC_ZA = (SEC_XBC[0] + SEC_XBC[1], ATT_WIDTH)
SEC_Q = (SEC_ZA[0] + SEC_ZA[1], ATT_WIDTH)
SEC_K = (SEC_Q[0] + SEC_Q[1], ATT_WIDTH)
SEC_V = (SEC_K[0] + SEC_K[1], ATT_WIDTH)
MAIN_COLS = SEC_V[0] + SEC_V[1]
TAIL_DT = 0
TAIL_F = SSD_HEADS
SPLIT_PARTS = 3
TAIL_ONE = TAIL_DT

IN_ROWS = 512
IN_COL_CHUNK = 512
SSD_ROWS = 256
ATT_TQ = 1024
ATT_TK = 512
OUT_ROWS = 1024
VMEM_LIMIT = 56 * 1024 * 1024


def _split_bf16(x, parts):
    out = []
    r = x
    for i in range(parts):
        h = r.astype(BF16)
        out.append(h)
        if i + 1 < parts:
            r = r - h.astype(F32)
    return out


def _softplus(x):
    return jnp.maximum(x, 0.0) + jnp.log1p(jnp.exp(-jnp.abs(x)))


def _silu(x):
    return x * jax.nn.sigmoid(x)


def _aug_lane(head, j):
    pair, odd = divmod(head, 2)
    return pair * LANES + (j if odd else HALF + j)


def _aug_matrices():
    eq = np.zeros((LANES, ATT_WIDTH), np.float32)
    ek = np.zeros((LANES, ATT_WIDTH), np.float32)
    for h in range(ATT_HEADS):
        for part in range(SPLIT_PARTS):
            src = TAIL_F + part * ATT_HEADS + h
            eq[src, _aug_lane(h, part)] = 1.0
            ek[src, _aug_lane(h, SPLIT_PARTS + part)] = -1.0
            eq[TAIL_ONE, _aug_lane(h, SPLIT_PARTS + part)] = 1.0
            ek[TAIL_ONE, _aug_lane(h, part)] = 1.0
    return jnp.asarray(eq, BF16), jnp.asarray(ek, BF16)


def _tri3():
    tri = np.tril(np.ones((CHUNK, CHUNK), np.float32))
    return jnp.asarray(np.concatenate([tri, tri, tri], axis=1), BF16)


def _cumsum_rows(tri3, v):
    parts = _split_bf16(v, 3)
    return jnp.dot(tri3, jnp.concatenate(parts, axis=0), preferred_element_type=F32)


def _in_proj_kernel(x_ref, g_ref, wm_ref, wt_ref, fgb_ref, tri3_ref, eq_ref, ek_ref,
                    zs_ref, xbc_ref, za_ref, qa_ref, ka_ref, va_ref, tail_ref, stats_ref,
                    carry_ref):
    @pl.when(pl.program_id(1) == 0)
    def _():
        carry_ref[...] = jnp.zeros_like(carry_ref)

    x = x_ref[...]
    ms = jnp.mean(x * x, axis=-1, keepdims=True)
    u = (x * lax.rsqrt(ms + EPS) * g_ref[...]).astype(BF16)

    def project(sec, c):
        return jnp.dot(u, wm_ref[:, sec[0] + c:sec[0] + c + IN_COL_CHUNK],
                       preferred_element_type=F32)

    def section(out_ref, sec):
        for c in range(0, sec[1], IN_COL_CHUNK):
            out_ref[:, c:c + IN_COL_CHUNK] = project(sec, c).astype(out_ref.dtype)

    tail = jnp.dot(u, wt_ref[...], preferred_element_type=F32)
    tail_ref[...] = tail

    tri3 = tri3_ref[...]
    part_lane = lax.broadcasted_iota(jnp.int32, (CHUNK, LANES), 1)
    carry = carry_ref[...]
    lhs = []
    for c in range(IN_ROWS // CHUNK):
        log_f = -_softplus(-(tail[c * CHUNK:(c + 1) * CHUNK] + fgb_ref[...]))
        cum = _cumsum_rows(tri3, log_f) + carry
        carry = cum[CHUNK - 1:CHUNK, :]
        if c == 0:
            c_first = cum[0:1, :] * LOG2E
        hi, mid, lo = _split_bf16(cum * LOG2E, SPLIT_PARTS)
        parts = jnp.where(part_lane < TAIL_F + ATT_HEADS, hi,
                          jnp.where(part_lane < TAIL_F + 2 * ATT_HEADS, mid, lo))
        lhs.append(jnp.where(part_lane == TAIL_ONE, jnp.ones_like(parts), parts))
    carry_ref[...] = carry
    lhs = jnp.concatenate(lhs, axis=0)
    aq = jnp.dot(lhs, eq_ref[...], preferred_element_type=F32).astype(BF16)
    ak = jnp.dot(lhs, ek_ref[...], preferred_element_type=F32).astype(BF16)

    section(zs_ref, SEC_ZS)
    section(xbc_ref, SEC_XBC)
    section(za_ref, SEC_ZA)

    lane = lax.broadcasted_iota(jnp.int32, (IN_ROWS, LANES), 1)
    low = lane < HALF
    one_even = jnp.where(lane == HALF, 1.0, 0.0).astype(BF16)
    one_odd = jnp.where(lane == 0, 1.0, 0.0).astype(BF16)

    head_lane = lax.broadcasted_iota(jnp.int32, (1, LANES), 1)

    def head_major(out_ref, sec, aug_even, aug_odd, mul=None, norms=False):
        sq_max = jnp.zeros((1, LANES), F32)
        for c in range(0, sec[1], IN_COL_CHUNK):
            acc = project(sec, c)
            if mul is not None:
                acc = acc * mul
            acc16 = acc.astype(BF16)
            for i in range(IN_COL_CHUNK // LANES):
                pair = c // LANES + i
                d2 = acc16[:, i * LANES:(i + 1) * LANES]
                out_ref[0, 2 * pair] = jnp.where(low, d2, aug_even(pair))
                out_ref[0, 2 * pair + 1] = jnp.where(low, aug_odd(pair), d2)
                if norms:
                    sq = acc[:, i * LANES:(i + 1) * LANES]
                    sq = sq * sq
                    for odd in (0, 1):
                        mine = jnp.logical_not(low) if odd else low
                        rows = jnp.sum(jnp.where(mine, sq, 0.0), axis=-1, keepdims=True)
                        sq_max = jnp.where(head_lane == 2 * pair + odd,
                                           jnp.max(rows, axis=0, keepdims=True), sq_max)
        return sq_max

    pair_cols = lambda a: (lambda pair: a[:, pair * LANES:(pair + 1) * LANES])
    q_sq = head_major(qa_ref, SEC_Q, pair_cols(aq), pair_cols(aq),
                      mul=ATT_HEAD_DIM ** -0.5 * LOG2E, norms=True)
    k_sq = head_major(ka_ref, SEC_K, pair_cols(ak), pair_cols(ak), norms=True)
    head_major(va_ref, SEC_V, lambda pair: one_even, lambda pair: one_odd)
    stats_ref[0] = jnp.concatenate(
        [q_sq, k_sq, c_first, carry * LOG2E, jnp.zeros((SUBLANES - 4, LANES), F32)], axis=0)


def _const_spec(a):
    zeros = (0,) * a.ndim
    return pl.BlockSpec(a.shape, lambda *_: zeros, pipeline_mode=pl.Buffered(1))


def _in_proj(x2, g, w_main, w_tail, fgb, bsz, seqlen):
    n = x2.shape[0]
    nt = seqlen // IN_ROWS
    tri3 = _tri3()
    eq, ek = _aug_matrices()
    row = lambda w: pl.BlockSpec((IN_ROWS, w), lambda b, t: (b * nt + t, 0))
    full = _const_spec
    head_major = pl.BlockSpec((1, ATT_HEADS, IN_ROWS, LANES), lambda b, t: (b, 0, t, 0))
    aug_shape = jax.ShapeDtypeStruct((bsz, ATT_HEADS, seqlen, LANES), BF16)
    out_shapes = (
        jax.ShapeDtypeStruct((n, SSD_WIDTH), BF16),
        jax.ShapeDtypeStruct((n, CONV_CH), BF16),
        jax.ShapeDtypeStruct((n, ATT_WIDTH), BF16),
        aug_shape, aug_shape, aug_shape,
        jax.ShapeDtypeStruct((n, LANES), F32),
        jax.ShapeDtypeStruct((bsz * nt, SUBLANES, LANES), F32),
    )
    return pl.pallas_call(
        _in_proj_kernel,
        out_shape=out_shapes,
        grid=(bsz, nt),
        in_specs=[row(D_MODEL), full(g), full(w_main), full(w_tail), full(fgb),
                  full(tri3), full(eq), full(ek)],
        out_specs=(row(SSD_WIDTH), row(CONV_CH), row(ATT_WIDTH),
                   head_major, head_major, head_major, row(LANES),
                   pl.BlockSpec((1, SUBLANES, LANES), lambda b, t: (b * nt + t, 0, 0))),
        scratch_shapes=[pltpu.VMEM((1, LANES), F32)],
        compiler_params=pltpu.CompilerParams(
            dimension_semantics=("arbitrary", "arbitrary"), vmem_limit_bytes=VMEM_LIMIT),
        name="in_proj",
    )(x2, g, w_main, w_tail, fgb, tri3, eq, ek)


def _head_expand2():
    e = np.zeros((2 * LANES, SSD_WIDTH), np.float32)
    for h in range(SSD_HEADS):
        e[TAIL_DT + h, h * SSD_HEAD_DIM:(h + 1) * SSD_HEAD_DIM] = 1.0
        e[LANES + TAIL_DT + h, h * SSD_HEAD_DIM:(h + 1) * SSD_HEAD_DIM] = 1.0
    return jnp.asarray(e, BF16)


def _conv_shift():
    s = np.zeros((CONV_WIDTH * CHUNK, CONV_HALO + CHUNK), np.float32)
    for j in range(CONV_WIDTH):
        for l in range(CHUNK):
            s[j * CHUNK + l, CONV_HALO + l - (CONV_WIDTH - 1) + j] = 1.0
    return jnp.asarray(s, BF16)


def _ssd_kernel(xbc_ref, tail_ref, z_ref, cw_ref, cb_ref, dtb_ref, a_ref, dfull_ref, g_ref,
                tri3_ref, e2_ref, shift_ref, xprev_ref, y_ref, xwin_ref, state_ref):
    @pl.when(pl.program_id(1) == 0)
    def _():
        state_ref[...] = jnp.zeros_like(state_ref)

    prev = xprev_ref[...]
    xwin_ref[0:CONV_HALO, :] = jnp.where(pl.program_id(1) > 0, prev, jnp.zeros_like(prev))
    xwin_ref[CONV_HALO:CONV_HALO + SSD_ROWS, :] = xbc_ref[...]

    tri3 = tri3_ref[...]
    row_i = lax.broadcasted_iota(jnp.int32, (CHUNK, CHUNK), 0)
    col_i = lax.broadcasted_iota(jnp.int32, (CHUNK, CHUNK), 1)
    causal = row_i >= col_i
    low = col_i < HALF

    for c in range(SSD_ROWS // CHUNK):
        r0 = c * CHUNK
        window = xwin_ref[r0:r0 + CONV_HALO + CHUNK, :]
        taps = jnp.dot(shift_ref[...], window, preferred_element_type=F32)
        conv = cb_ref[...]
        for j in range(CONV_WIDTH):
            conv = conv + cw_ref[j:j + 1, :] * taps[j * CHUNK:(j + 1) * CHUNK]
        xc = _silu(conv)
        xs = xc[:, :SSD_WIDTH]
        bm = xc[:, SSD_WIDTH:SSD_WIDTH + N_GROUPS * D_STATE].astype(BF16)
        cm = xc[:, SSD_WIDTH + N_GROUPS * D_STATE:].astype(BF16)

        dt = _softplus(tail_ref[r0:r0 + CHUNK, :] + dtb_ref[...])
        adt = dt * a_ref[...]
        acs = _cumsum_rows(tri3, adt)
        acs_t = acs.T
        last = acs[CHUNK - 1:CHUNK, :]
        w = dt * jnp.exp(last - acs)
        ea = jnp.exp(acs)
        cd = jnp.broadcast_to(jnp.exp(last), (BF16_ROWS, LANES))
        stack = jnp.concatenate([dt, w, ea, cd], axis=0)
        lhs = jnp.concatenate(_split_bf16(stack, 2), axis=1)
        spread = jnp.dot(lhs, e2_ref[...], preferred_element_type=F32)
        dt_full = spread[0:CHUNK]
        w_full = spread[CHUNK:2 * CHUNK]
        ea_full = spread[2 * CHUNK:3 * CHUNK]
        cd_full = spread[3 * CHUNK:3 * CHUNK + 1]

        xdt = (xs * dt_full).astype(BF16)
        wx = (xs * w_full).astype(BF16)

        y_blocks = []
        yoff_blocks = []
        for g in range(N_GROUPS):
            gs = slice(g * D_STATE, (g + 1) * D_STATE)
            cb_g = lax.dot_general(cm[:, gs], bm[:, gs], (((1,), (1,)), ((), ())),
                                   preferred_element_type=F32)
            for jp in range(HEADS_PER_GROUP // 2):
                pair = g * (HEADS_PER_GROUP // 2) + jp
                x_pair = xdt[:, pair * LANES:(pair + 1) * LANES]
                ys = []
                for r in (2 * pair, 2 * pair + 1):
                    seg = acs[:, r:r + 1] - acs_t[r:r + 1, :]
                    dec = jnp.exp(jnp.where(causal, seg, -jnp.inf))
                    m = (cb_g * dec).astype(BF16)
                    ys.append(jnp.dot(m, x_pair, preferred_element_type=F32))
                y_blocks.append(jnp.where(low, ys[0], ys[1]))
            hs = slice(g * GROUP_WIDTH, (g + 1) * GROUP_WIDTH)
            st = state_ref[:, hs]
            yoff_blocks.append(jnp.dot(cm[:, gs], st.astype(BF16), preferred_element_type=F32))
            s_g = lax.dot_general(bm[:, gs], wx[:, hs], (((0,), (0,)), ((), ())),
                                  preferred_element_type=F32)
            state_ref[:, hs] = st * cd_full[:, hs] + s_g
        y = jnp.concatenate(y_blocks, axis=1)
        y_off = jnp.concatenate(yoff_blocks, axis=1)
        y = y + y_off * ea_full + dfull_ref[...] * xs
        y = y * _silu(z_ref[r0:r0 + CHUNK, :].astype(F32))
        outs = []
        for g in range(N_GROUPS):
            yg = y[:, g * GROUP_WIDTH:(g + 1) * GROUP_WIDTH]
            ms = jnp.mean(yg * yg, axis=-1, keepdims=True)
            outs.append(yg * lax.rsqrt(ms + EPS))
        yn = jnp.concatenate(outs, axis=1) * g_ref[...]
        y_ref[r0:r0 + CHUNK, :] = yn.astype(y_ref.dtype)


def _ssd(xbc, tail, z_ssd, conv_w, conv_b, dtb, a_neg, d_full, g, bsz, seqlen):
    nt = seqlen // SSD_ROWS
    tri3 = _tri3()
    e2 = _head_expand2()
    shift = _conv_shift()
    row = lambda w: pl.BlockSpec((SSD_ROWS, w), lambda b, t: (b * nt + t, 0))
    full = _const_spec
    halo_per_tile = SSD_ROWS // CONV_HALO
    prev = pl.BlockSpec(
        (CONV_HALO, CONV_CH),
        lambda b, t: (jnp.maximum((b * nt + t) * halo_per_tile - 1, 0), 0))
    return pl.pallas_call(
        _ssd_kernel,
        out_shape=jax.ShapeDtypeStruct((bsz * seqlen, SSD_WIDTH), BF16),
        grid=(bsz, nt),
        in_specs=[row(CONV_CH), row(LANES), row(SSD_WIDTH), full(conv_w), full(conv_b),
                  full(dtb), full(a_neg), full(d_full), full(g), full(tri3), full(e2),
                  full(shift), prev],
        out_specs=row(SSD_WIDTH),
        scratch_shapes=[pltpu.VMEM((CONV_HALO + SSD_ROWS, CONV_CH), BF16),
                        pltpu.VMEM((D_STATE, SSD_WIDTH), F32)],
        compiler_params=pltpu.CompilerParams(
            dimension_semantics=("arbitrary", "arbitrary"), vmem_limit_bytes=VMEM_LIMIT),
        name="ssd",
    )(xbc, tail, z_ssd, conv_w, conv_b, dtb, a_neg, d_full, g, tri3, e2, shift, xbc)


def _fox_attn_kernel(skip_ref, qa_ref, ka_ref, va_ref, z_ref, g_ref, o_ref, s_ref, acc_ref,
                     m_ref):
    qi = pl.program_id(2)
    lane = lax.broadcasted_iota(jnp.int32, (ATT_TQ, LANES), 1)
    low = lane < HALF
    nt_dims = (((1,), (1,)), ((), ()))
    n_diag = ATT_TQ // ATT_TK
    all_rows = (0, ATT_TQ)

    def scores(hh, kstart, rows=all_rows):
        k = ka_ref[0, hh, pl.ds(kstart, ATT_TK), :]
        s_ref[hh, rows[0]:rows[1], :] = lax.dot_general(
            qa_ref[0, hh, rows[0]:rows[1], :], k, nt_dims, preferred_element_type=F32)

    def softmax_pv(hh, kstart, rows=all_rows, diag=None):
        r0, r1 = rows
        s = s_ref[hh, r0:r1, :]
        if diag is not None:
            row_i = lax.broadcasted_iota(jnp.int32, (r1 - r0, ATT_TK), 0) + r0
            col_i = lax.broadcasted_iota(jnp.int32, (r1 - r0, ATT_TK), 1) + diag * ATT_TK
            s = jnp.where(row_i >= col_i, s, -jnp.inf)
        m_old = m_ref[hh, r0:r1, :]
        m_new = jnp.maximum(m_old, jnp.max(s, axis=-1, keepdims=True))
        p = jnp.exp2(s - jnp.tile(m_new, (1, ATT_TK // LANES))).astype(BF16)
        alpha = jnp.exp2(m_old - m_new)
        m_ref[hh, r0:r1, :] = m_new
        v = va_ref[0, hh, pl.ds(kstart, ATT_TK), :]
        acc_ref[hh, r0:r1, :] = (alpha * acc_ref[hh, r0:r1, :]
                                 + jnp.dot(p, v, preferred_element_type=F32))

    first = skip_ref[(pl.program_id(0) * pl.num_programs(1) + pl.program_id(1))
                     * pl.num_programs(2) + qi]
    m_ref[...] = jnp.full(m_ref.shape, NEG_BIG, F32)
    acc_ref[...] = jnp.zeros(acc_ref.shape, F32)
    scores(0, pl.multiple_of(first * ATT_TK, ATT_TK))

    def key_tile_step(kstart):
        scores(1, kstart)
        softmax_pv(0, kstart)
        scores(0, kstart + ATT_TK)
        softmax_pv(1, kstart)

    @pl.when(first % 2 == 1)
    def _():
        key_tile_step(pl.multiple_of(first * ATT_TK, ATT_TK))

    def body(j, carry):
        for u in range(n_diag):
            key_tile_step(pl.multiple_of(j * ATT_TQ, ATT_TQ) + u * ATT_TK)
        return carry

    lax.fori_loop((first + 1) // 2, qi, body, 0)
    kdiag = pl.multiple_of(qi * ATT_TQ, ATT_TQ)
    for d in range(n_diag):
        rows = (d * ATT_TK, ATT_TQ)
        scores(1, kdiag + d * ATT_TK, rows)
        softmax_pv(0, kdiag + d * ATT_TK, rows, diag=d)
        if d + 1 < n_diag:
            scores(0, kdiag + (d + 1) * ATT_TK, ((d + 1) * ATT_TK, ATT_TQ))
        softmax_pv(1, kdiag + d * ATT_TK, rows, diag=d)

    def normed(hh):
        acc = acc_ref[hh]
        l_lane = 0 if hh else HALF
        mine = jnp.logical_not(low) if hh else low
        l = jnp.sum(jnp.where(lane == l_lane, acc, 0.0), axis=-1, keepdims=True)
        ssq = jnp.sum(jnp.where(mine, acc * acc, 0.0), axis=-1, keepdims=True)
        inv_l = 1.0 / l
        return acc * (inv_l * lax.rsqrt(ssq * (inv_l * inv_l) * (1.0 / ATT_HEAD_DIM) + EPS))

    att = jnp.where(low, normed(0), normed(1)) * g_ref[...]
    o_ref[...] = (att * _silu(z_ref[...].astype(F32))).astype(o_ref.dtype)


def _skip_table(stats, bsz, seqlen):
    assert ATT_TK == IN_ROWS and ATT_TQ == 2 * ATT_TK
    nt = seqlen // ATT_TK
    nq = seqlen // ATT_TQ
    per_q = ATT_TQ // ATT_TK
    st = stats.reshape(bsz, nt, SUBLANES, LANES)
    qn = jnp.sqrt(st[:, :, 0, :ATT_HEADS]) * NORM_SLACK
    kn = jnp.sqrt(st[:, :, 1, :ATT_HEADS]) * NORM_SLACK
    c_first = st[:, :, 2, TAIL_F:TAIL_F + ATT_HEADS]
    c_last = st[:, :, 3, TAIL_F:TAIL_F + ATT_HEADS]
    by_q = lambda a: a.reshape(bsz, nq, per_q, ATT_HEADS)
    qmax = by_q(qn).max(axis=2)[:, :, None, :]
    kdiag = by_q(kn).max(axis=2)[:, :, None, :]
    decay = c_last[:, None, :, :] - by_q(c_first)[:, :, 0, None, :]
    key_tile = jnp.arange(nt, dtype=jnp.int32)[None, None, :, None]
    before = key_tile < (jnp.arange(nq, dtype=jnp.int32) * per_q)[None, :, None, None]
    dead = (qmax * (kn[:, None, :, :] + kdiag) + SKIP_MARGIN < decay) & before
    lead = jnp.min(jnp.where(dead, nt, key_tile), axis=2)
    lead = lead.reshape(bsz, nq, ATT_HEADS // 2, 2).min(axis=-1)
    return jnp.transpose(lead, (0, 2, 1)).reshape(-1).astype(jnp.int32)


def _fox_attn(qa, ka, va, z_att, g2, skip):
    bsz, _, seqlen, _ = qa.shape
    nq = seqlen // ATT_TQ
    q_spec = pl.BlockSpec((1, 2, ATT_TQ, LANES), lambda b, p, i, s: (b, p, i, 0))
    kv_spec = pl.BlockSpec((1, 2, seqlen, LANES), lambda b, p, i, s: (b, p, 0, 0))
    row_spec = pl.BlockSpec((ATT_TQ, LANES), lambda b, p, i, s: (b * nq + i, p))
    return pl.pallas_call(
        _fox_attn_kernel,
        out_shape=jax.ShapeDtypeStruct((bsz * seqlen, ATT_WIDTH), BF16),
        grid_spec=pltpu.PrefetchScalarGridSpec(
            num_scalar_prefetch=1,
            grid=(bsz, ATT_HEADS // 2, nq),
            in_specs=[q_spec, kv_spec, kv_spec, row_spec,
                      pl.BlockSpec(g2.shape, lambda b, p, i, s: (0, 0))],
            out_specs=row_spec,
            scratch_shapes=[pltpu.VMEM((2, ATT_TQ, ATT_TK), F32),
                            pltpu.VMEM((2, ATT_TQ, LANES), F32),
                            pltpu.VMEM((2, ATT_TQ, LANES), F32)]),
        compiler_params=pltpu.CompilerParams(
            dimension_semantics=("arbitrary", "arbitrary", "arbitrary"),
            vmem_limit_bytes=VMEM_LIMIT),
        name="fox_attn",
    )(skip, qa, ka, va, z_att, g2)


def _rms(h, g):
    ms = jnp.mean(h * h, axis=-1, keepdims=True)
    return h * lax.rsqrt(ms + EPS) * g


def _out_ple_kernel(x_ref, ys_ref, ya_ref, p_ref, wo_s_ref, wo_a_ref, wg_ref, wp_ref,
                    gp_ref, gf_ref, o_ref):
    h = x_ref[...]
    h = h + (jnp.dot(ys_ref[...], wo_s_ref[...], preferred_element_type=F32)
             + jnp.dot(ya_ref[...], wo_a_ref[...], preferred_element_type=F32))
    hn = _rms(h, gp_ref[...]).astype(BF16)
    gate = jax.nn.sigmoid(jnp.dot(hn, wg_ref[...], preferred_element_type=F32))
    emb = jnp.dot(p_ref[...].astype(BF16), wp_ref[...], preferred_element_type=F32)
    h = h + gate * emb
    o_ref[...] = _rms(h, gf_ref[...])


def _out_ple(x2, y_ssd, y_att, p2, wo_s, wo_a, wg, wp, gp, gf):
    n = x2.shape[0]
    row = lambda w: pl.BlockSpec((OUT_ROWS, w), lambda i: (i, 0))
    full = _const_spec
    return pl.pallas_call(
        _out_ple_kernel,
        out_shape=jax.ShapeDtypeStruct((n, D_MODEL), F32),
        grid=(n // OUT_ROWS,),
        in_specs=[row(D_MODEL), row(SSD_WIDTH), row(ATT_WIDTH), row(PLE_DIM),
                  full(wo_s), full(wo_a), full(wg), full(wp), full(gp), full(gf)],
        out_specs=row(D_MODEL),
        compiler_params=pltpu.CompilerParams(
            dimension_semantics=("arbitrary",), vmem_limit_bytes=VMEM_LIMIT),
        name="out_ple",
    )(x2, y_ssd, y_att, p2, wo_s, wo_a, wg, wp, gp, gf)


def _layer(h2, p2, bsz, seqlen, norm_g, w_in, conv_w, conv_b, dt_bias, a_log, d_skip,
           ssd_norm_g, fg_bias, att_norm_g, w_out, ple_norm_g, w_ple_gate, w_ple_proj, out_g):
    o_zs = 0
    o_xbc = o_zs + SSD_WIDTH
    o_dt = o_xbc + CONV_CH
    o_za = o_dt + SSD_HEADS
    o_q = o_za + ATT_WIDTH
    o_k = o_q + ATT_WIDTH
    o_v = o_k + ATT_WIDTH
    o_f = o_v + ATT_WIDTH
    w_in16 = w_in.astype(BF16)
    order = jnp.argsort(fg_bias)
    fg_bias = fg_bias[order]

    def heads_cols(w):
        return w.reshape(D_MODEL, ATT_HEADS, ATT_HEAD_DIM)[:, order].reshape(D_MODEL, ATT_WIDTH)

    att_cols = [heads_cols(w_in16[:, o:o + ATT_WIDTH]) for o in (o_za, o_q, o_k, o_v)]
    w_main = jnp.concatenate([w_in16[:, o_zs:o_dt]] + att_cols, axis=1)
    w_f = w_in16[:, o_f:o_f + ATT_HEADS][:, order]
    w_tail = jnp.concatenate(
        [w_in16[:, o_dt:o_za]] + [w_f] * SPLIT_PARTS
        + [jnp.zeros((D_MODEL, LANES - SSD_HEADS - SPLIT_PARTS * ATT_HEADS), BF16)], axis=1)
    w_out_att = w_out[SSD_WIDTH:].astype(BF16).reshape(
        ATT_HEADS, ATT_HEAD_DIM, D_MODEL)[order].reshape(ATT_WIDTH, D_MODEL)

    def lanes16(v, off):
        return jnp.zeros((1, LANES), F32).at[0, off:off + v.shape[0]].set(v.astype(F32))

    z_ssd, xbc, z_att, qa, ka, va, tail, stats = _in_proj(
        h2, norm_g.reshape(1, D_MODEL).astype(F32), w_main, w_tail,
        lanes16(jnp.tile(fg_bias, SPLIT_PARTS), TAIL_F), bsz, seqlen)

    y_ssd = _ssd(
        xbc, tail, z_ssd, conv_w.astype(F32), conv_b.reshape(1, CONV_CH).astype(F32),
        lanes16(dt_bias, TAIL_DT), lanes16(-jnp.exp(a_log.astype(F32)), TAIL_DT),
        jnp.repeat(d_skip.astype(F32), SSD_HEAD_DIM).reshape(1, SSD_WIDTH),
        ssd_norm_g.reshape(1, SSD_WIDTH).astype(F32), bsz, seqlen)

    y_att = _fox_attn(qa, ka, va, z_att,
                      jnp.tile(att_norm_g.astype(F32), 2).reshape(1, LANES),
                      _skip_table(stats, bsz, seqlen))

    return _out_ple(
        h2, y_ssd, y_att, p2,
        w_out[:SSD_WIDTH].astype(BF16), w_out_att,
        w_ple_gate.astype(BF16), w_ple_proj.astype(BF16),
        ple_norm_g.reshape(1, D_MODEL).astype(F32), out_g.reshape(1, D_MODEL).astype(F32))


def kernel(x, p, norm_g, w_in, conv_w, conv_b, dt_bias, a_log, d_skip, ssd_norm_g, fg_bias,
           att_norm_g, w_out, ple_norm_g, w_ple_gate, w_ple_proj, final_norm_g):
    bsz, seqlen, _ = x.shape
    depth = p.shape[0]
    assert depth == 1, "the fused tail applies the final norm right after the only layer"
    h2 = x.reshape(bsz * seqlen, D_MODEL)
    out = _layer(h2, p.reshape(bsz * seqlen, PLE_DIM), bsz, seqlen,
                 norm_g[0], w_in[0], conv_w[0], conv_b[0], dt_bias[0], a_log[0], d_skip[0],
                 ssd_norm_g[0], fg_bias[0], att_norm_g[0], w_out[0], ple_norm_g[0],
                 w_ple_gate[0], w_ple_proj[0], final_norm_g)
    return out.reshape(bsz, seqlen, D_MODEL)
```

```python
import functools
import math

import jax
import jax.numpy as jnp
import numpy as np
from jax import lax
from jax.experimental import pallas as pl
from jax.experimental.pallas import tpu as pltpu

F32 = jnp.float32
BF16 = jnp.bfloat16

D_MODEL = 1024
SSD_WIDTH = 1024
ATT_WIDTH = 1024
SSD_HEAD_DIM = 64
SSD_HEADS = 16
N_GROUPS = 2
HEADS_PER_GROUP = SSD_HEADS // N_GROUPS
GROUP_WIDTH = SSD_WIDTH // N_GROUPS
D_STATE = 128
CONV_WIDTH = 4
CHUNK = 128
ATT_HEAD_DIM = 64
ATT_HEADS = 16
PLE_DIM = 256
EPS = 1e-6
CONV_CH = SSD_WIDTH + 2 * N_GROUPS * D_STATE

LANES = 128
SUBLANES = 8
HALF = LANES // 2
BF16_ROWS = 16
CONV_HALO = 128

LOG2E = 1.4426950408889634
NEG_BIG = -1e30
SKIP_MARGIN = 136.0
NORM_SLACK = 1.01

SEC_ZS = (0, 0, SSD_WIDTH)
SEC_XBC = (0, SSD_WIDTH, CONV_CH)
SEC_ZA = (1, 0, ATT_WIDTH)
SEC_Q = (2, 0, ATT_WIDTH)
SEC_K = (3, 0, ATT_WIDTH)
SEC_V = (4, 0, ATT_WIDTH)
TAIL_DT = 0
TAIL_F = SSD_HEADS
SPLIT_PARTS = 3
TAIL_ONE = TAIL_DT

IN_ROWS = 512
IN_COL_CHUNK = 512
SSD_ROWS = 512
ATT_TQ = 1024
ATT_TK = 512
OUT_ROWS = 1024
VMEM_LIMIT = 56 * 1024 * 1024


def _split_bf16(x, parts):
    out = []
    r = x
    for i in range(parts):
        h = r.astype(BF16)
        out.append(h)
        if i + 1 < parts:
            r = r - h.astype(F32)
    return out


def _softplus(x):
    return jnp.maximum(x, 0.0) + jnp.log1p(jnp.exp(-jnp.abs(x)))


def _silu(x):
    return x * jax.nn.sigmoid(x)


def _aug_lane(head, j):
    pair, odd = divmod(head, 2)
    return pair * LANES + (j if odd else HALF + j)


def _aug_matrices():
    eq = np.zeros((LANES, ATT_WIDTH), np.float32)
    ek = np.zeros((LANES, ATT_WIDTH), np.float32)
    for h in range(ATT_HEADS):
        for part in range(SPLIT_PARTS):
            src = TAIL_F + part * ATT_HEADS + h
            eq[src, _aug_lane(h, part)] = 1.0
            ek[src, _aug_lane(h, SPLIT_PARTS + part)] = -1.0
            eq[TAIL_ONE, _aug_lane(h, SPLIT_PARTS + part)] = 1.0
            ek[TAIL_ONE, _aug_lane(h, part)] = 1.0
    return jnp.asarray(eq, BF16), jnp.asarray(ek, BF16)


def _tri3():
    tri = np.tril(np.ones((CHUNK, CHUNK), np.float32))
    return jnp.asarray(np.concatenate([tri, tri, tri], axis=1), BF16)


def _cumsum_rows(tri3, v):
    parts = _split_bf16(v, 3)
    return jnp.dot(tri3, jnp.concatenate(parts, axis=0), preferred_element_type=F32)


def _in_proj_kernel(x_ref, g_ref, ws_ref, wza_ref, wq_ref, wk_ref, wv_ref, wt_ref, fgb_ref,
                    tri3_ref, eq_ref, ek_ref,
                    zs_ref, xbc_ref, za_ref, qa_ref, ka_ref, va_ref, tail_ref, stats_ref,
                    carry_ref):
    w_refs = (ws_ref, wza_ref, wq_ref, wk_ref, wv_ref)

    @pl.when(pl.program_id(1) == 0)
    def _():
        carry_ref[...] = jnp.zeros_like(carry_ref)

    x = x_ref[...]
    ms = jnp.mean(x * x, axis=-1, keepdims=True)
    u = (x * lax.rsqrt(ms + EPS) * g_ref[...]).astype(BF16)

    def project(sec, c):
        return jnp.dot(u, w_refs[sec[0]][:, sec[1] + c:sec[1] + c + IN_COL_CHUNK],
                       preferred_element_type=F32)

    def section(out_ref, sec):
        for c in range(0, sec[2], IN_COL_CHUNK):
            out_ref[:, c:c + IN_COL_CHUNK] = project(sec, c).astype(out_ref.dtype)

    tail = jnp.dot(u, wt_ref[...], preferred_element_type=F32)
    tail_ref[...] = tail

    tri3 = tri3_ref[...]
    part_lane = lax.broadcasted_iota(jnp.int32, (CHUNK, LANES), 1)
    carry = carry_ref[...]
    lhs = []
    for c in range(IN_ROWS // CHUNK):
        log_f = -_softplus(-(tail[c * CHUNK:(c + 1) * CHUNK] + fgb_ref[...]))
        cum = _cumsum_rows(tri3, log_f) + carry
        carry = cum[CHUNK - 1:CHUNK, :]
        if c == 0:
            c_first = cum[0:1, :] * LOG2E
        hi, mid, lo = _split_bf16(cum * LOG2E, SPLIT_PARTS)
        parts = jnp.where(part_lane < TAIL_F + ATT_HEADS, hi,
                          jnp.where(part_lane < TAIL_F + 2 * ATT_HEADS, mid, lo))
        lhs.append(jnp.where(part_lane == TAIL_ONE, jnp.ones_like(parts), parts))
    carry_ref[...] = carry
    lhs = jnp.concatenate(lhs, axis=0)
    aq = jnp.dot(lhs, eq_ref[...], preferred_element_type=F32).astype(BF16)
    ak = jnp.dot(lhs, ek_ref[...], preferred_element_type=F32).astype(BF16)

    section(zs_ref, SEC_ZS)
    section(xbc_ref, SEC_XBC)
    section(za_ref, SEC_ZA)

    lane = lax.broadcasted_iota(jnp.int32, (IN_ROWS, LANES), 1)
    low = lane < HALF
    one_even = jnp.where(lane == HALF, 1.0, 0.0).astype(BF16)
    one_odd = jnp.where(lane == 0, 1.0, 0.0).astype(BF16)

    head_lane = lax.broadcasted_iota(jnp.int32, (1, LANES), 1)

    def head_major(out_ref, sec, aug_even, aug_odd, mul=None, norms=False):
        sq_max = jnp.zeros((1, LANES), F32)
        for c in range(0, sec[2], IN_COL_CHUNK):
            acc = project(sec, c)
            if mul is not None:
                acc = acc * mul
            acc16 = acc.astype(BF16)
            for i in range(IN_COL_CHUNK // LANES):
                pair = c // LANES + i
                d2 = acc16[:, i * LANES:(i + 1) * LANES]
                out_ref[0, 2 * pair] = jnp.where(low, d2, aug_even(pair))
                out_ref[0, 2 * pair + 1] = jnp.where(low, aug_odd(pair), d2)
                if norms:
                    sq = acc[:, i * LANES:(i + 1) * LANES]
                    sq = sq * sq
                    for odd in (0, 1):
                        mine = jnp.logical_not(low) if odd else low
                        rows = jnp.sum(jnp.where(mine, sq, 0.0), axis=-1, keepdims=True)
                        sq_max = jnp.where(head_lane == 2 * pair + odd,
                                           jnp.max(rows, axis=0, keepdims=True), sq_max)
        return sq_max

    pair_cols = lambda a: (lambda pair: a[:, pair * LANES:(pair + 1) * LANES])
    q_sq = head_major(qa_ref, SEC_Q, pair_cols(aq), pair_cols(aq),
                      mul=ATT_HEAD_DIM ** -0.5 * LOG2E, norms=True)
    k_sq = head_major(ka_ref, SEC_K, pair_cols(ak), pair_cols(ak), norms=True)
    head_major(va_ref, SEC_V, lambda pair: one_even, lambda pair: one_odd)
    stats_ref[0] = jnp.concatenate(
        [q_sq, k_sq, c_first, carry * LOG2E, jnp.zeros((SUBLANES - 4, LANES), F32)], axis=0)


def _const_spec(a):
    zeros = (0,) * a.ndim
    return pl.BlockSpec(a.shape, lambda *_: zeros, pipeline_mode=pl.Buffered(1))


def _in_proj(x2, g, w_secs, w_tail, fgb, bsz, seqlen):
    n = x2.shape[0]
    nt = seqlen // IN_ROWS
    tri3 = _tri3()
    eq, ek = _aug_matrices()
    row = lambda w: pl.BlockSpec((IN_ROWS, w), lambda b, t: (b * nt + t, 0))
    full = _const_spec
    head_major = pl.BlockSpec((1, ATT_HEADS, IN_ROWS, LANES), lambda b, t: (b, 0, t, 0))
    aug_shape = jax.ShapeDtypeStruct((bsz, ATT_HEADS, seqlen, LANES), BF16)
    out_shapes = (
        jax.ShapeDtypeStruct((n, SSD_WIDTH), BF16),
        jax.ShapeDtypeStruct((n, CONV_CH), BF16),
        jax.ShapeDtypeStruct((n, ATT_WIDTH), BF16),
        aug_shape, aug_shape, aug_shape,
        jax.ShapeDtypeStruct((n, LANES), F32),
        jax.ShapeDtypeStruct((bsz * nt, SUBLANES, LANES), F32),
    )
    return pl.pallas_call(
        _in_proj_kernel,
        out_shape=out_shapes,
        grid=(bsz, nt),
        in_specs=[row(D_MODEL), full(g)] + [full(w) for w in w_secs]
        + [full(w_tail), full(fgb), full(tri3), full(eq), full(ek)],
        out_specs=(row(SSD_WIDTH), row(CONV_CH), row(ATT_WIDTH),
                   head_major, head_major, head_major, row(LANES),
                   pl.BlockSpec((1, SUBLANES, LANES), lambda b, t: (b * nt + t, 0, 0))),
        scratch_shapes=[pltpu.VMEM((1, LANES), F32)],
        compiler_params=pltpu.CompilerParams(
            dimension_semantics=("arbitrary", "arbitrary"), vmem_limit_bytes=VMEM_LIMIT),
        name="in_proj",
    )(x2, g, *w_secs, w_tail, fgb, tri3, eq, ek)


def _head_expand2():
    e = np.zeros((2 * LANES, SSD_WIDTH), np.float32)
    for h in range(SSD_HEADS):
        e[TAIL_DT + h, h * SSD_HEAD_DIM:(h + 1) * SSD_HEAD_DIM] = 1.0
        e[LANES + TAIL_DT + h, h * SSD_HEAD_DIM:(h + 1) * SSD_HEAD_DIM] = 1.0
    return jnp.asarray(e, BF16)


def _conv_shift():
    s = np.zeros((CONV_WIDTH * CHUNK, CONV_HALO + CHUNK), np.float32)
    for j in range(CONV_WIDTH):
        for l in range(CHUNK):
            s[j * CHUNK + l, CONV_HALO + l - (CONV_WIDTH - 1) + j] = 1.0
    return jnp.asarray(s, BF16)


def _ssd_kernel(xbc_ref, tail_ref, z_ref, cw_ref, cb_ref, dtb_ref, a_ref, dfull_ref, g_ref,
                tri3_ref, e2_ref, shift_ref, xprev_ref, y_ref, xwin_ref, state_ref):
    @pl.when(pl.program_id(1) == 0)
    def _():
        state_ref[...] = jnp.zeros_like(state_ref)

    prev = xprev_ref[...]
    xwin_ref[0:CONV_HALO, :] = jnp.where(pl.program_id(1) > 0, prev, jnp.zeros_like(prev))
    xwin_ref[CONV_HALO:CONV_HALO + SSD_ROWS, :] = xbc_ref[...]

    tri3 = tri3_ref[...]
    row_i = lax.broadcasted_iota(jnp.int32, (CHUNK, CHUNK), 0)
    col_i = lax.broadcasted_iota(jnp.int32, (CHUNK, CHUNK), 1)
    causal = row_i >= col_i
    low = col_i < HALF

    for c in range(SSD_ROWS // CHUNK):
        r0 = c * CHUNK
        window = xwin_ref[r0:r0 + CONV_HALO + CHUNK, :]
        taps = jnp.dot(shift_ref[...], window, preferred_element_type=F32)
        conv = cb_ref[...]
        for j in range(CONV_WIDTH):
            conv = conv + cw_ref[j:j + 1, :] * taps[j * CHUNK:(j + 1) * CHUNK]
        xc = _silu(conv)
        xs = xc[:, :SSD_WIDTH]
        bm = xc[:, SSD_WIDTH:SSD_WIDTH + N_GROUPS * D_STATE].astype(BF16)
        cm = xc[:, SSD_WIDTH + N_GROUPS * D_STATE:].astype(BF16)

        dt = _softplus(tail_ref[r0:r0 + CHUNK, :] + dtb_ref[...])
        adt = dt * a_ref[...]
        acs = _cumsum_rows(tri3, adt)
        acs_t = acs.T
        last = acs[CHUNK - 1:CHUNK, :]
        w = dt * jnp.exp(last - acs)
        ea = jnp.exp(acs)
        cd = jnp.broadcast_to(jnp.exp(last), (BF16_ROWS, LANES))
        stack = jnp.concatenate([dt, w, ea, cd], axis=0)
        lhs = jnp.concatenate(_split_bf16(stack, 2), axis=1)
        spread = jnp.dot(lhs, e2_ref[...], preferred_element_type=F32)
        dt_full = spread[0:CHUNK]
        w_full = spread[CHUNK:2 * CHUNK]
        ea_full = spread[2 * CHUNK:3 * CHUNK]
        cd_full = spread[3 * CHUNK:3 * CHUNK + 1]

        xdt = (xs * dt_full).astype(BF16)
        wx = (xs * w_full).astype(BF16)

        y_blocks = []
        yoff_blocks = []
        for g in range(N_GROUPS):
            gs = slice(g * D_STATE, (g + 1) * D_STATE)
            cb_g = lax.dot_general(cm[:, gs], bm[:, gs], (((1,), (1,)), ((), ())),
                                   preferred_element_type=F32)
            for jp in range(HEADS_PER_GROUP // 2):
                pair = g * (HEADS_PER_GROUP // 2) + jp
                x_pair = xdt[:, pair * LANES:(pair + 1) * LANES]
                ys = []
                for r in (2 * pair, 2 * pair + 1):
                    seg = acs[:, r:r + 1] - acs_t[r:r + 1, :]
                    dec = jnp.exp(jnp.where(causal, seg, -jnp.inf))
                    m = (cb_g * dec).astype(BF16)
                    ys.append(jnp.dot(m, x_pair, preferred_element_type=F32))
                y_blocks.append(jnp.where(low, ys[0], ys[1]))
            hs = slice(g * GROUP_WIDTH, (g + 1) * GROUP_WIDTH)
            st = state_ref[:, hs]
            yoff_blocks.append(jnp.dot(cm[:, gs], st.astype(BF16), preferred_element_type=F32))
            s_g = lax.dot_general(bm[:, gs], wx[:, hs], (((0,), (0,)), ((), ())),
                                  preferred_element_type=F32)
            state_ref[:, hs] = st * cd_full[:, hs] + s_g
        y = jnp.concatenate(y_blocks, axis=1)
        y_off = jnp.concatenate(yoff_blocks, axis=1)
        y = y + y_off * ea_full + dfull_ref[...] * xs
        y = y * _silu(z_ref[r0:r0 + CHUNK, :].astype(F32))
        outs = []
        for g in range(N_GROUPS):
            yg = y[:, g * GROUP_WIDTH:(g + 1) * GROUP_WIDTH]
            ms = jnp.mean(yg * yg, axis=-1, keepdims=True)
            outs.append(yg * lax.rsqrt(ms + EPS))
        yn = jnp.concatenate(outs, axis=1) * g_ref[...]
        y_ref[r0:r0 + CHUNK, :] = yn.astype(y_ref.dtype)


def _ssd(xbc, tail, z_ssd, conv_w, conv_b, dtb, a_neg, d_full, g, bsz, seqlen):
    nt = seqlen // SSD_ROWS
    tri3 = _tri3()
    e2 = _head_expand2()
    shift = _conv_shift()
    row = lambda w: pl.BlockSpec((SSD_ROWS, w), lambda b, t: (b * nt + t, 0))
    full = _const_spec
    halo_per_tile = SSD_ROWS // CONV_HALO
    prev = pl.BlockSpec(
        (CONV_HALO, CONV_CH),
        lambda b, t: (jnp.maximum((b * nt + t) * halo_per_tile - 1, 0), 0))
    return pl.pallas_call(
        _ssd_kernel,
        out_shape=jax.ShapeDtypeStruct((bsz * seqlen, SSD_WIDTH), BF16),
        grid=(bsz, nt),
        in_specs=[row(CONV_CH), row(LANES), row(SSD_WIDTH), full(conv_w), full(conv_b),
                  full(dtb), full(a_neg), full(d_full), full(g), full(tri3), full(e2),
                  full(shift), prev],
        out_specs=row(SSD_WIDTH),
        scratch_shapes=[pltpu.VMEM((CONV_HALO + SSD_ROWS, CONV_CH), BF16),
                        pltpu.VMEM((D_STATE, SSD_WIDTH), F32)],
        compiler_params=pltpu.CompilerParams(
            dimension_semantics=("arbitrary", "arbitrary"), vmem_limit_bytes=VMEM_LIMIT),
        name="ssd",
    )(xbc, tail, z_ssd, conv_w, conv_b, dtb, a_neg, d_full, g, tri3, e2, shift, xbc)


def _fox_attn_kernel(skip_ref, qa_ref, ka_ref, va_ref, z_ref, g_ref, o_ref, s_ref, acc_ref,
                     m_ref):
    qi = pl.program_id(2)
    lane = lax.broadcasted_iota(jnp.int32, (ATT_TQ, LANES), 1)
    low = lane < HALF
    nt_dims = (((1,), (1,)), ((), ()))
    n_diag = ATT_TQ // ATT_TK
    all_rows = (0, ATT_TQ)

    def scores(hh, kstart, rows=all_rows):
        k = ka_ref[0, hh, pl.ds(kstart, ATT_TK), :]
        s_ref[hh, rows[0]:rows[1], :] = lax.dot_general(
            qa_ref[0, hh, rows[0]:rows[1], :], k, nt_dims, preferred_element_type=F32)

    def softmax_pv(hh, kstart, rows=all_rows, diag=None):
        r0, r1 = rows
        s = s_ref[hh, r0:r1, :]
        if diag is not None:
            row_i = lax.broadcasted_iota(jnp.int32, (r1 - r0, ATT_TK), 0) + r0
            col_i = lax.broadcasted_iota(jnp.int32, (r1 - r0, ATT_TK), 1) + diag * ATT_TK
            s = jnp.where(row_i >= col_i, s, -jnp.inf)
        m_old = m_ref[hh, r0:r1, :]
        m_new = jnp.maximum(m_old, jnp.max(s, axis=-1, keepdims=True))
        p = jnp.exp2(s - jnp.tile(m_new, (1, ATT_TK // LANES))).astype(BF16)
        alpha = jnp.exp2(m_old - m_new)
        m_ref[hh, r0:r1, :] = m_new
        v = va_ref[0, hh, pl.ds(kstart, ATT_TK), :]
        acc_ref[hh, r0:r1, :] = (alpha * acc_ref[hh, r0:r1, :]
                                 + jnp.dot(p, v, preferred_element_type=F32))

    first = skip_ref[(pl.program_id(0) * pl.num_programs(1) + pl.program_id(1))
                     * pl.num_programs(2) + qi]
    m_ref[...] = jnp.full(m_ref.shape, NEG_BIG, F32)
    acc_ref[...] = jnp.zeros(acc_ref.shape, F32)
    scores(0, pl.multiple_of(first * ATT_TK, ATT_TK))

    def key_tile_step(kstart):
        scores(1, kstart)
        softmax_pv(0, kstart)
        scores(0, kstart + ATT_TK)
        softmax_pv(1, kstart)

    @pl.when(first % 2 == 1)
    def _():
        key_tile_step(pl.multiple_of(first * ATT_TK, ATT_TK))

    def body(j, carry):
        for u in range(n_diag):
            key_tile_step(pl.multiple_of(j * ATT_TQ, ATT_TQ) + u * ATT_TK)
        return carry

    lax.fori_loop((first + 1) // 2, qi, body, 0)
    kdiag = pl.multiple_of(qi * ATT_TQ, ATT_TQ)
    for d in range(n_diag):
        rows = (d * ATT_TK, ATT_TQ)
        scores(1, kdiag + d * ATT_TK, rows)
        softmax_pv(0, kdiag + d * ATT_TK, rows, diag=d)
        if d + 1 < n_diag:
            scores(0, kdiag + (d + 1) * ATT_TK, ((d + 1) * ATT_TK, ATT_TQ))
        softmax_pv(1, kdiag + d * ATT_TK, rows, diag=d)

    def normed(hh):
        acc = acc_ref[hh]
        l_lane = 0 if hh else HALF
        mine = jnp.logical_not(low) if hh else low
        l = jnp.sum(jnp.where(lane == l_lane, acc, 0.0), axis=-1, keepdims=True)
        ssq = jnp.sum(jnp.where(mine, acc * acc, 0.0), axis=-1, keepdims=True)
        inv_l = 1.0 / l
        return acc * (inv_l * lax.rsqrt(ssq * (inv_l * inv_l) * (1.0 / ATT_HEAD_DIM) + EPS))

    att = jnp.where(low, normed(0), normed(1)) * g_ref[...]
    o_ref[...] = (att * _silu(z_ref[...].astype(F32))).astype(o_ref.dtype)


def _skip_table(stats, bsz, seqlen):
    assert ATT_TK == IN_ROWS and ATT_TQ == 2 * ATT_TK
    nt = seqlen // ATT_TK
    nq = seqlen // ATT_TQ
    per_q = ATT_TQ // ATT_TK
    st = stats.reshape(bsz, nt, SUBLANES, LANES)
    qn = jnp.sqrt(st[:, :, 0, :ATT_HEADS]) * NORM_SLACK
    kn = jnp.sqrt(st[:, :, 1, :ATT_HEADS]) * NORM_SLACK
    c_first = st[:, :, 2, TAIL_F:TAIL_F + ATT_HEADS]
    c_last = st[:, :, 3, TAIL_F:TAIL_F + ATT_HEADS]
    by_q = lambda a: a.reshape(bsz, nq, per_q, ATT_HEADS)
    qmax = by_q(qn).max(axis=2)[:, :, None, :]
    kdiag = by_q(kn).max(axis=2)[:, :, None, :]
    decay = c_last[:, None, :, :] - by_q(c_first)[:, :, 0, None, :]
    key_tile = jnp.arange(nt, dtype=jnp.int32)[None, None, :, None]
    before = key_tile < (jnp.arange(nq, dtype=jnp.int32) * per_q)[None, :, None, None]
    dead = (qmax * (kn[:, None, :, :] + kdiag) + SKIP_MARGIN < decay) & before
    lead = jnp.min(jnp.where(dead, nt, key_tile), axis=2)
    lead = lead.reshape(bsz, nq, ATT_HEADS // 2, 2).min(axis=-1)
    return jnp.transpose(lead, (0, 2, 1)).reshape(-1).astype(jnp.int32)


def _fox_attn(qa, ka, va, z_att, g2, skip):
    bsz, _, seqlen, _ = qa.shape
    nq = seqlen // ATT_TQ
    q_spec = pl.BlockSpec((1, 2, ATT_TQ, LANES), lambda b, p, i, s: (b, p, i, 0))
    kv_spec = pl.BlockSpec((1, 2, seqlen, LANES), lambda b, p, i, s: (b, p, 0, 0))
    row_spec = pl.BlockSpec((ATT_TQ, LANES), lambda b, p, i, s: (b * nq + i, p))
    return pl.pallas_call(
        _fox_attn_kernel,
        out_shape=jax.ShapeDtypeStruct((bsz * seqlen, ATT_WIDTH), BF16),
        grid_spec=pltpu.PrefetchScalarGridSpec(
            num_scalar_prefetch=1,
            grid=(bsz, ATT_HEADS // 2, nq),
            in_specs=[q_spec, kv_spec, kv_spec, row_spec,
                      pl.BlockSpec(g2.shape, lambda b, p, i, s: (0, 0))],
            out_specs=row_spec,
            scratch_shapes=[pltpu.VMEM((2, ATT_TQ, ATT_TK), F32),
                            pltpu.VMEM((2, ATT_TQ, LANES), F32),
                            pltpu.VMEM((2, ATT_TQ, LANES), F32)]),
        compiler_params=pltpu.CompilerParams(
            dimension_semantics=("arbitrary", "arbitrary", "arbitrary"),
            vmem_limit_bytes=VMEM_LIMIT),
        name="fox_attn",
    )(skip, qa, ka, va, z_att, g2)


def _rms(h, g):
    ms = jnp.mean(h * h, axis=-1, keepdims=True)
    return h * lax.rsqrt(ms + EPS) * g


def _out_ple_kernel(x_ref, ys_ref, ya_ref, p_ref, wo_s_ref, wo_a_ref, wg_ref, wp_ref,
                    gp_ref, gf_ref, o_ref):
    h = x_ref[...]
    h = h + (jnp.dot(ys_ref[...], wo_s_ref[...], preferred_element_type=F32)
             + jnp.dot(ya_ref[...], wo_a_ref[...], preferred_element_type=F32))
    hn = _rms(h, gp_ref[...]).astype(BF16)
    gate = jax.nn.sigmoid(jnp.dot(hn, wg_ref[...], preferred_element_type=F32))
    emb = jnp.dot(p_ref[...].astype(BF16), wp_ref[...], preferred_element_type=F32)
    h = h + gate * emb
    o_ref[...] = _rms(h, gf_ref[...])


def _out_ple(x2, y_ssd, y_att, p2, wo_s, wo_a, wg, wp, gp, gf):
    n = x2.shape[0]
    row = lambda w: pl.BlockSpec((OUT_ROWS, w), lambda i: (i, 0))
    full = _const_spec
    return pl.pallas_call(
        _out_ple_kernel,
        out_shape=jax.ShapeDtypeStruct((n, D_MODEL), F32),
        grid=(n // OUT_ROWS,),
        in_specs=[row(D_MODEL), row(SSD_WIDTH), row(ATT_WIDTH), row(PLE_DIM),
                  full(wo_s), full(wo_a), full(wg), full(wp), full(gp), full(gf)],
        out_specs=row(D_MODEL),
        compiler_params=pltpu.CompilerParams(
            dimension_semantics=("arbitrary",), vmem_limit_bytes=VMEM_LIMIT),
        name="out_ple",
    )(x2, y_ssd, y_att, p2, wo_s, wo_a, wg, wp, gp, gf)


def _layer(h2, p2, bsz, seqlen, norm_g, w_in, conv_w, conv_b, dt_bias, a_log, d_skip,
           ssd_norm_g, fg_bias, att_norm_g, w_out, ple_norm_g, w_ple_gate, w_ple_proj, out_g):
    o_zs = 0
    o_xbc = o_zs + SSD_WIDTH
    o_dt = o_xbc + CONV_CH
    o_za = o_dt + SSD_HEADS
    o_q = o_za + ATT_WIDTH
    o_k = o_q + ATT_WIDTH
    o_v = o_k + ATT_WIDTH
    o_f = o_v + ATT_WIDTH
    w_in16 = w_in.astype(BF16)
    order = jnp.argsort(fg_bias)
    fg_bias = fg_bias[order]

    def heads_cols(w):
        return w.reshape(D_MODEL, ATT_HEADS, ATT_HEAD_DIM)[:, order].reshape(D_MODEL, ATT_WIDTH)

    w_secs = [w_in16[:, o_zs:o_dt]] + [
        heads_cols(w_in16[:, o:o + ATT_WIDTH]) for o in (o_za, o_q, o_k, o_v)]
    w_f = w_in16[:, o_f:o_f + ATT_HEADS][:, order]
    w_tail = jnp.concatenate(
        [w_in16[:, o_dt:o_za]] + [w_f] * SPLIT_PARTS
        + [jnp.zeros((D_MODEL, LANES - SSD_HEADS - SPLIT_PARTS * ATT_HEADS), BF16)], axis=1)
    w_out_att = w_out[SSD_WIDTH:].astype(BF16).reshape(
        ATT_HEADS, ATT_HEAD_DIM, D_MODEL)[order].reshape(ATT_WIDTH, D_MODEL)

    def lanes16(v, off):
        return jnp.zeros((1, LANES), F32).at[0, off:off + v.shape[0]].set(v.astype(F32))

    z_ssd, xbc, z_att, qa, ka, va, tail, stats = _in_proj(
        h2, norm_g.reshape(1, D_MODEL).astype(F32), w_secs, w_tail,
        lanes16(jnp.tile(fg_bias, SPLIT_PARTS), TAIL_F), bsz, seqlen)

    y_ssd = _ssd(
        xbc, tail, z_ssd, conv_w.astype(F32), conv_b.reshape(1, CONV_CH).astype(F32),
        lanes16(dt_bias, TAIL_DT), lanes16(-jnp.exp(a_log.astype(F32)), TAIL_DT),
        jnp.repeat(d_skip.astype(F32), SSD_HEAD_DIM).reshape(1, SSD_WIDTH),
        ssd_norm_g.reshape(1, SSD_WIDTH).astype(F32), bsz, seqlen)

    y_att = _fox_attn(qa, ka, va, z_att,
                      jnp.tile(att_norm_g.astype(F32), 2).reshape(1, LANES),
                      _skip_table(stats, bsz, seqlen))

    return _out_ple(
        h2, y_ssd, y_att, p2,
        w_out[:SSD_WIDTH].astype(BF16), w_out_att,
        w_ple_gate.astype(BF16), w_ple_proj.astype(BF16),
        ple_norm_g.reshape(1, D_MODEL).astype(F32), out_g.reshape(1, D_MODEL).astype(F32))


def kernel(x, p, norm_g, w_in, conv_w, conv_b, dt_bias, a_log, d_skip, ssd_norm_g, fg_bias,
           att_norm_g, w_out, ple_norm_g, w_ple_gate, w_ple_proj, final_norm_g):
    bsz, seqlen, _ = x.shape
    depth = p.shape[0]
    assert depth == 1, "the fused tail applies the final norm right after the only layer"
    h2 = x.reshape(bsz * seqlen, D_MODEL)
    out = _layer(h2, p.reshape(bsz * seqlen, PLE_DIM), bsz, seqlen,
                 norm_g[0], w_in[0], conv_w[0], conv_b[0], dt_bias[0], a_log[0], d_skip[0],
                 ssd_norm_g[0], fg_bias[0], att_norm_g[0], w_out[0], ple_norm_g[0],
                 w_ple_gate[0], w_ple_proj[0], final_norm_g)
    return out.reshape(bsz, seqlen, D_MODEL)
```

```python
import jax
import jax.numpy as jnp
import numpy as np
from jax import lax
from jax.experimental import pallas as pl
from jax.experimental.pallas import tpu as pltpu

F32 = jnp.float32
BF16 = jnp.bfloat16

D_MODEL = 1024
SSD_WIDTH = 1024
ATT_WIDTH = 1024
SSD_HEAD_DIM = 64
SSD_HEADS = 16
N_GROUPS = 2
HEADS_PER_GROUP = SSD_HEADS // N_GROUPS
GROUP_WIDTH = SSD_WIDTH // N_GROUPS
D_STATE = 128
CONV_WIDTH = 4
CHUNK = 128
ATT_HEAD_DIM = 64
ATT_HEADS = 16
PLE_DIM = 256
EPS = 1e-6
CONV_CH = SSD_WIDTH + 2 * N_GROUPS * D_STATE

LANES = 128
SUBLANES = 8
HALF = LANES // 2
BF16_ROWS = 16
CONV_HALO = 128
V7X_VMEM_BYTES = 64 * 1024 * 1024

LOG2E = 1.4426950408889634
NEG_BIG = -1e30
SKIP_MARGIN = 136.0
NORM_SLACK = 1.01

SEC_ZS = (0, 0, SSD_WIDTH)
SEC_XBC = (0, SSD_WIDTH, CONV_CH)
SEC_ZA = (1, 0, ATT_WIDTH)
SEC_Q = (2, 0, ATT_WIDTH)
SEC_K = (3, 0, ATT_WIDTH)
SEC_V = (4, 0, ATT_WIDTH)
TAIL_DT = 0
TAIL_F = SSD_HEADS
SPLIT_PARTS = 3
TAIL_ONE = TAIL_DT

IN_ROWS = 512
IN_COL_CHUNK = 512
SSD_ROWS = 512
ATT_TQ = 1024
ATT_TK = 512
OUT_ROWS = 1024


def _nbytes(shape, dtype):
    return int(np.prod(shape)) * jnp.dtype(dtype).itemsize


def _vmem_limit(pipelined, resident, temporaries):
    need = (2 * sum(_nbytes(*a) for a in pipelined) + sum(_nbytes(*a) for a in resident)
            + sum(_nbytes(*a) for a in temporaries))
    assert need <= V7X_VMEM_BYTES, need
    return need


def _shape_dtype(a):
    return (a.shape, a.dtype)


def _split_bf16(x, parts):
    out = []
    r = x
    for i in range(parts):
        h = r.astype(BF16)
        out.append(h)
        if i + 1 < parts:
            r = r - h.astype(F32)
    return out


def _softplus(x):
    return jnp.maximum(x, 0.0) + jnp.log1p(jnp.exp(-jnp.abs(x)))


def _silu(x):
    return x * jax.nn.sigmoid(x)


def _aug_lane(head, j):
    pair, odd = divmod(head, 2)
    return pair * LANES + (j if odd else HALF + j)


def _aug_matrices():
    eq = np.zeros((LANES, ATT_WIDTH), np.float32)
    ek = np.zeros((LANES, ATT_WIDTH), np.float32)
    for h in range(ATT_HEADS):
        for part in range(SPLIT_PARTS):
            src = TAIL_F + part * ATT_HEADS + h
            eq[src, _aug_lane(h, part)] = 1.0
            ek[src, _aug_lane(h, SPLIT_PARTS + part)] = -1.0
            eq[TAIL_ONE, _aug_lane(h, SPLIT_PARTS + part)] = 1.0
            ek[TAIL_ONE, _aug_lane(h, part)] = 1.0
    return jnp.asarray(eq, BF16), jnp.asarray(ek, BF16)


def _tri3():
    tri = np.tril(np.ones((CHUNK, CHUNK), np.float32))
    return jnp.asarray(np.concatenate([tri, tri, tri], axis=1), BF16)


def _cumsum_rows(tri3, v):
    parts = _split_bf16(v, 3)
    return jnp.dot(tri3, jnp.concatenate(parts, axis=0), preferred_element_type=F32)


def _in_proj_kernel(x_ref, g_ref, ws_ref, wza_ref, wq_ref, wk_ref, wv_ref, wt_ref, fgb_ref,
                    tri3_ref, eq_ref, ek_ref,
                    zs_ref, xbc_ref, za_ref, qa_ref, ka_ref, va_ref, tail_ref, stats_ref,
                    carry_ref):
    w_refs = (ws_ref, wza_ref, wq_ref, wk_ref, wv_ref)

    @pl.when(pl.program_id(1) == 0)
    def _():
        carry_ref[...] = jnp.zeros_like(carry_ref)

    x = x_ref[...]
    ms = jnp.mean(x * x, axis=-1, keepdims=True)
    u = (x * lax.rsqrt(ms + EPS) * g_ref[...]).astype(BF16)

    def project(sec, c):
        return jnp.dot(u, w_refs[sec[0]][:, sec[1] + c:sec[1] + c + IN_COL_CHUNK],
                       preferred_element_type=F32)

    def section(out_ref, sec):
        for c in range(0, sec[2], IN_COL_CHUNK):
            out_ref[:, c:c + IN_COL_CHUNK] = project(sec, c).astype(out_ref.dtype)

    tail = jnp.dot(u, wt_ref[...], preferred_element_type=F32)
    tail_ref[...] = tail

    tri3 = tri3_ref[...]
    part_lane = lax.broadcasted_iota(jnp.int32, (CHUNK, LANES), 1)
    carry = carry_ref[...]
    lhs = []
    for c in range(IN_ROWS // CHUNK):
        log_f = -_softplus(-(tail[c * CHUNK:(c + 1) * CHUNK] + fgb_ref[...]))
        cum = _cumsum_rows(tri3, log_f) + carry
        carry = cum[CHUNK - 1:CHUNK, :]
        if c == 0:
            c_first = cum[0:1, :] * LOG2E
        hi, mid, lo = _split_bf16(cum * LOG2E, SPLIT_PARTS)
        parts = jnp.where(part_lane < TAIL_F + ATT_HEADS, hi,
                          jnp.where(part_lane < TAIL_F + 2 * ATT_HEADS, mid, lo))
        lhs.append(jnp.where(part_lane == TAIL_ONE, jnp.ones_like(parts), parts))
    carry_ref[...] = carry
    lhs = jnp.concatenate(lhs, axis=0)
    aq = jnp.dot(lhs, eq_ref[...], preferred_element_type=F32).astype(BF16)
    ak = jnp.dot(lhs, ek_ref[...], preferred_element_type=F32).astype(BF16)

    section(zs_ref, SEC_ZS)
    section(xbc_ref, SEC_XBC)
    section(za_ref, SEC_ZA)

    lane = lax.broadcasted_iota(jnp.int32, (IN_ROWS, LANES), 1)
    low = lane < HALF
    one_even = jnp.where(lane == HALF, 1.0, 0.0).astype(BF16)
    one_odd = jnp.where(lane == 0, 1.0, 0.0).astype(BF16)

    head_lane = lax.broadcasted_iota(jnp.int32, (1, LANES), 1)

    def head_major(out_ref, sec, aug_even, aug_odd, mul=None, norms=False):
        sq_max = jnp.zeros((1, LANES), F32)
        for c in range(0, sec[2], IN_COL_CHUNK):
            acc = project(sec, c)
            if mul is not None:
                acc = acc * mul
            acc16 = acc.astype(BF16)
            for i in range(IN_COL_CHUNK // LANES):
                pair = c // LANES + i
                d2 = acc16[:, i * LANES:(i + 1) * LANES]
                out_ref[0, 2 * pair] = jnp.where(low, d2, aug_even(pair))
                out_ref[0, 2 * pair + 1] = jnp.where(low, aug_odd(pair), d2)
                if norms:
                    sq = acc[:, i * LANES:(i + 1) * LANES]
                    sq = sq * sq
                    for odd in (0, 1):
                        mine = jnp.logical_not(low) if odd else low
                        rows = jnp.sum(jnp.where(mine, sq, 0.0), axis=-1, keepdims=True)
                        sq_max = jnp.where(head_lane == 2 * pair + odd,
                                           jnp.max(rows, axis=0, keepdims=True), sq_max)
        return sq_max

    pair_cols = lambda a: (lambda pair: a[:, pair * LANES:(pair + 1) * LANES])
    q_sq = head_major(qa_ref, SEC_Q, pair_cols(aq), pair_cols(aq),
                      mul=ATT_HEAD_DIM ** -0.5 * LOG2E, norms=True)
    k_sq = head_major(ka_ref, SEC_K, pair_cols(ak), pair_cols(ak), norms=True)
    head_major(va_ref, SEC_V, lambda pair: one_even, lambda pair: one_odd)
    stats_ref[0] = jnp.concatenate(
        [q_sq, k_sq, c_first, carry * LOG2E, jnp.zeros((SUBLANES - 4, LANES), F32)], axis=0)


def _const_spec(a):
    zeros = (0,) * a.ndim
    return pl.BlockSpec(a.shape, lambda *_: zeros, pipeline_mode=pl.Buffered(1))


def _in_proj(x2, g, w_secs, w_tail, fgb, bsz, seqlen):
    n = x2.shape[0]
    nt = seqlen // IN_ROWS
    tri3 = _tri3()
    eq, ek = _aug_matrices()
    row = lambda w: pl.BlockSpec((IN_ROWS, w), lambda b, t: (b * nt + t, 0))
    full = _const_spec
    head_major = pl.BlockSpec((1, ATT_HEADS, IN_ROWS, LANES), lambda b, t: (b, 0, t, 0))
    aug_shape = jax.ShapeDtypeStruct((bsz, ATT_HEADS, seqlen, LANES), BF16)
    out_shapes = (
        jax.ShapeDtypeStruct((n, SSD_WIDTH), BF16),
        jax.ShapeDtypeStruct((n, CONV_CH), BF16),
        jax.ShapeDtypeStruct((n, ATT_WIDTH), BF16),
        aug_shape, aug_shape, aug_shape,
        jax.ShapeDtypeStruct((n, LANES), F32),
        jax.ShapeDtypeStruct((bsz * nt, SUBLANES, LANES), F32),
    )
    resident = [g, *w_secs, w_tail, fgb, tri3, eq, ek]
    vmem = _vmem_limit(
        pipelined=[((IN_ROWS, D_MODEL), F32), ((IN_ROWS, SSD_WIDTH + CONV_CH + ATT_WIDTH), BF16),
                   ((3 * ATT_HEADS, IN_ROWS, LANES), BF16), ((IN_ROWS, LANES), F32)],
        resident=[_shape_dtype(a) for a in resident],
        temporaries=[((IN_ROWS, D_MODEL), BF16), ((2 * IN_ROWS, ATT_WIDTH), BF16),
                     ((8 * IN_ROWS, IN_COL_CHUNK), F32)])
    return pl.pallas_call(
        _in_proj_kernel,
        out_shape=out_shapes,
        grid=(bsz, nt),
        in_specs=[row(D_MODEL)] + [full(a) for a in resident],
        out_specs=(row(SSD_WIDTH), row(CONV_CH), row(ATT_WIDTH),
                   head_major, head_major, head_major, row(LANES),
                   pl.BlockSpec((1, SUBLANES, LANES), lambda b, t: (b * nt + t, 0, 0))),
        scratch_shapes=[pltpu.VMEM((1, LANES), F32)],
        compiler_params=pltpu.CompilerParams(
            dimension_semantics=("arbitrary", "arbitrary"), vmem_limit_bytes=vmem),
        name="in_proj",
    )(x2, *resident)


def _head_expand2():
    e = np.zeros((2 * LANES, SSD_WIDTH), np.float32)
    for h in range(SSD_HEADS):
        e[TAIL_DT + h, h * SSD_HEAD_DIM:(h + 1) * SSD_HEAD_DIM] = 1.0
        e[LANES + TAIL_DT + h, h * SSD_HEAD_DIM:(h + 1) * SSD_HEAD_DIM] = 1.0
    return jnp.asarray(e, BF16)


def _conv_shift():
    s = np.zeros((CONV_WIDTH * CHUNK, CONV_HALO + CHUNK), np.float32)
    for j in range(CONV_WIDTH):
        for l in range(CHUNK):
            s[j * CHUNK + l, CONV_HALO + l - (CONV_WIDTH - 1) + j] = 1.0
    return jnp.asarray(s, BF16)


def _ssd_kernel(xbc_ref, tail_ref, z_ref, cw_ref, cb_ref, dtb_ref, a_ref, dfull_ref, g_ref,
                tri3_ref, e2_ref, shift_ref, xprev_ref, y_ref, xwin_ref, state_ref):
    @pl.when(pl.program_id(1) == 0)
    def _():
        state_ref[...] = jnp.zeros_like(state_ref)

    prev = xprev_ref[...]
    xwin_ref[0:CONV_HALO, :] = jnp.where(pl.program_id(1) > 0, prev, jnp.zeros_like(prev))
    xwin_ref[CONV_HALO:CONV_HALO + SSD_ROWS, :] = xbc_ref[...]

    tri3 = tri3_ref[...]
    row_i = lax.broadcasted_iota(jnp.int32, (CHUNK, CHUNK), 0)
    col_i = lax.broadcasted_iota(jnp.int32, (CHUNK, CHUNK), 1)
    causal = row_i >= col_i
    low = col_i < HALF

    for c in range(SSD_ROWS // CHUNK):
        r0 = c * CHUNK
        window = xwin_ref[r0:r0 + CONV_HALO + CHUNK, :]
        taps = jnp.dot(shift_ref[...], window, preferred_element_type=F32)
        conv = cb_ref[...]
        for j in range(CONV_WIDTH):
            conv = conv + cw_ref[j:j + 1, :] * taps[j * CHUNK:(j + 1) * CHUNK]
        xc = _silu(conv)
        xs = xc[:, :SSD_WIDTH]
        bm = xc[:, SSD_WIDTH:SSD_WIDTH + N_GROUPS * D_STATE].astype(BF16)
        cm = xc[:, SSD_WIDTH + N_GROUPS * D_STATE:].astype(BF16)

        dt = _softplus(tail_ref[r0:r0 + CHUNK, :] + dtb_ref[...])
        adt = dt * a_ref[...]
        acs = _cumsum_rows(tri3, adt)
        acs_t = acs.T
        last = acs[CHUNK - 1:CHUNK, :]
        w = dt * jnp.exp(last - acs)
        ea = jnp.exp(acs)
        cd = jnp.broadcast_to(jnp.exp(last), (BF16_ROWS, LANES))
        stack = jnp.concatenate([dt, w, ea, cd], axis=0)
        lhs = jnp.concatenate(_split_bf16(stack, 2), axis=1)
        spread = jnp.dot(lhs, e2_ref[...], preferred_element_type=F32)
        dt_full = spread[0:CHUNK]
        w_full = spread[CHUNK:2 * CHUNK]
        ea_full = spread[2 * CHUNK:3 * CHUNK]
        cd_full = spread[3 * CHUNK:3 * CHUNK + 1]

        xdt = (xs * dt_full).astype(BF16)
        wx = (xs * w_full).astype(BF16)

        y_blocks = []
        yoff_blocks = []
        for g in range(N_GROUPS):
            gs = slice(g * D_STATE, (g + 1) * D_STATE)
            cb_g = lax.dot_general(cm[:, gs], bm[:, gs], (((1,), (1,)), ((), ())),
                                   preferred_element_type=F32)
            for jp in range(HEADS_PER_GROUP // 2):
                pair = g * (HEADS_PER_GROUP // 2) + jp
                x_pair = xdt[:, pair * LANES:(pair + 1) * LANES]
                ys = []
                for r in (2 * pair, 2 * pair + 1):
                    seg = acs[:, r:r + 1] - acs_t[r:r + 1, :]
                    dec = jnp.exp(jnp.where(causal, seg, -jnp.inf))
                    m = (cb_g * dec).astype(BF16)
                    ys.append(jnp.dot(m, x_pair, preferred_element_type=F32))
                y_blocks.append(jnp.where(low, ys[0], ys[1]))
            hs = slice(g * GROUP_WIDTH, (g + 1) * GROUP_WIDTH)
            st = state_ref[:, hs]
            yoff_blocks.append(jnp.dot(cm[:, gs], st.astype(BF16), preferred_element_type=F32))
            s_g = lax.dot_general(bm[:, gs], wx[:, hs], (((0,), (0,)), ((), ())),
                                  preferred_element_type=F32)
            state_ref[:, hs] = st * cd_full[:, hs] + s_g
        y = jnp.concatenate(y_blocks, axis=1)
        y_off = jnp.concatenate(yoff_blocks, axis=1)
        y = y + y_off * ea_full + dfull_ref[...] * xs
        y = y * _silu(z_ref[r0:r0 + CHUNK, :].astype(F32))
        outs = []
        for g in range(N_GROUPS):
            yg = y[:, g * GROUP_WIDTH:(g + 1) * GROUP_WIDTH]
            ms = jnp.mean(yg * yg, axis=-1, keepdims=True)
            outs.append(yg * lax.rsqrt(ms + EPS))
        yn = jnp.concatenate(outs, axis=1) * g_ref[...]
        y_ref[r0:r0 + CHUNK, :] = yn.astype(y_ref.dtype)


def _ssd(xbc, tail, z_ssd, conv_w, conv_b, dtb, a_neg, d_full, g, bsz, seqlen):
    nt = seqlen // SSD_ROWS
    tri3 = _tri3()
    e2 = _head_expand2()
    shift = _conv_shift()
    row = lambda w: pl.BlockSpec((SSD_ROWS, w), lambda b, t: (b * nt + t, 0))
    full = _const_spec
    halo_per_tile = SSD_ROWS // CONV_HALO
    prev = pl.BlockSpec(
        (CONV_HALO, CONV_CH),
        lambda b, t: (jnp.maximum((b * nt + t) * halo_per_tile - 1, 0), 0))
    resident = [conv_w, conv_b, dtb, a_neg, d_full, g, tri3, e2, shift]
    xwin = ((CONV_HALO + SSD_ROWS, CONV_CH), BF16)
    state = ((D_STATE, SSD_WIDTH), F32)
    vmem = _vmem_limit(
        pipelined=[((SSD_ROWS + CONV_HALO, CONV_CH), BF16), ((SSD_ROWS, LANES), F32),
                   ((2 * SSD_ROWS, SSD_WIDTH), BF16)],
        resident=[_shape_dtype(a) for a in resident] + [xwin, state],
        temporaries=[((CONV_WIDTH * CHUNK + 2 * CHUNK, CONV_CH), F32),
                     ((400 + 6 * CHUNK, SSD_WIDTH), F32)] * (SSD_ROWS // CHUNK))
    return pl.pallas_call(
        _ssd_kernel,
        out_shape=jax.ShapeDtypeStruct((bsz * seqlen, SSD_WIDTH), BF16),
        grid=(bsz, nt),
        in_specs=[row(CONV_CH), row(LANES), row(SSD_WIDTH)] + [full(a) for a in resident]
        + [prev],
        out_specs=row(SSD_WIDTH),
        scratch_shapes=[pltpu.VMEM(*xwin), pltpu.VMEM(*state)],
        compiler_params=pltpu.CompilerParams(
            dimension_semantics=("arbitrary", "arbitrary"), vmem_limit_bytes=vmem),
        name="ssd",
    )(xbc, tail, z_ssd, *resident, xbc)


def _fox_attn_kernel(skip_ref, qa_ref, ka_ref, va_ref, z_ref, g_ref, o_ref, s_ref, acc_ref,
                     m_ref):
    qi = pl.program_id(2)
    lane = lax.broadcasted_iota(jnp.int32, (ATT_TQ, LANES), 1)
    low = lane < HALF
    nt_dims = (((1,), (1,)), ((), ()))
    n_diag = ATT_TQ // ATT_TK
    all_rows = (0, ATT_TQ)

    def scores(hh, kstart, rows=all_rows):
        k = ka_ref[0, hh, pl.ds(kstart, ATT_TK), :]
        s_ref[hh, rows[0]:rows[1], :] = lax.dot_general(
            qa_ref[0, hh, rows[0]:rows[1], :], k, nt_dims, preferred_element_type=F32)

    def softmax_pv(hh, kstart, rows=all_rows, diag=None):
        r0, r1 = rows
        s = s_ref[hh, r0:r1, :]
        if diag is not None:
            row_i = lax.broadcasted_iota(jnp.int32, (r1 - r0, ATT_TK), 0) + r0
            col_i = lax.broadcasted_iota(jnp.int32, (r1 - r0, ATT_TK), 1) + diag * ATT_TK
            s = jnp.where(row_i >= col_i, s, -jnp.inf)
        m_old = m_ref[hh, r0:r1, :]
        m_new = jnp.maximum(m_old, jnp.max(s, axis=-1, keepdims=True))
        p = jnp.exp2(s - jnp.tile(m_new, (1, ATT_TK // LANES))).astype(BF16)
        alpha = jnp.exp2(m_old - m_new)
        m_ref[hh, r0:r1, :] = m_new
        v = va_ref[0, hh, pl.ds(kstart, ATT_TK), :]
        acc_ref[hh, r0:r1, :] = (alpha * acc_ref[hh, r0:r1, :]
                                 + jnp.dot(p, v, preferred_element_type=F32))

    first = skip_ref[(pl.program_id(0) * pl.num_programs(1) + pl.program_id(1))
                     * pl.num_programs(2) + qi]
    m_ref[...] = jnp.full(m_ref.shape, NEG_BIG, F32)
    acc_ref[...] = jnp.zeros(acc_ref.shape, F32)
    scores(0, pl.multiple_of(first * ATT_TK, ATT_TK))

    def key_tile_step(kstart):
        scores(1, kstart)
        softmax_pv(0, kstart)
        scores(0, kstart + ATT_TK)
        softmax_pv(1, kstart)

    @pl.when(first % 2 == 1)
    def _():
        key_tile_step(pl.multiple_of(first * ATT_TK, ATT_TK))

    def body(j, carry):
        for u in range(n_diag):
            key_tile_step(pl.multiple_of(j * ATT_TQ, ATT_TQ) + u * ATT_TK)
        return carry

    lax.fori_loop((first + 1) // 2, qi, body, 0)
    kdiag = pl.multiple_of(qi * ATT_TQ, ATT_TQ)
    for d in range(n_diag):
        rows = (d * ATT_TK, ATT_TQ)
        scores(1, kdiag + d * ATT_TK, rows)
        softmax_pv(0, kdiag + d * ATT_TK, rows, diag=d)
        if d + 1 < n_diag:
            scores(0, kdiag + (d + 1) * ATT_TK, ((d + 1) * ATT_TK, ATT_TQ))
        softmax_pv(1, kdiag + d * ATT_TK, rows, diag=d)

    def normed(hh):
        acc = acc_ref[hh]
        l_lane = 0 if hh else HALF
        mine = jnp.logical_not(low) if hh else low
        l = jnp.sum(jnp.where(lane == l_lane, acc, 0.0), axis=-1, keepdims=True)
        ssq = jnp.sum(jnp.where(mine, acc * acc, 0.0), axis=-1, keepdims=True)
        inv_l = 1.0 / l
        return acc * (inv_l * lax.rsqrt(ssq * (inv_l * inv_l) * (1.0 / ATT_HEAD_DIM) + EPS))

    att = jnp.where(low, normed(0), normed(1)) * g_ref[...]
    o_ref[...] = (att * _silu(z_ref[...].astype(F32))).astype(o_ref.dtype)


def _skip_table(stats, bsz, seqlen):
    assert ATT_TK == IN_ROWS and ATT_TQ == 2 * ATT_TK
    nt = seqlen // ATT_TK
    nq = seqlen // ATT_TQ
    per_q = ATT_TQ // ATT_TK
    st = stats.reshape(bsz, nt, SUBLANES, LANES)
    qn = jnp.sqrt(st[:, :, 0, :ATT_HEADS]) * NORM_SLACK
    kn = jnp.sqrt(st[:, :, 1, :ATT_HEADS]) * NORM_SLACK
    c_first = st[:, :, 2, TAIL_F:TAIL_F + ATT_HEADS]
    c_last = st[:, :, 3, TAIL_F:TAIL_F + ATT_HEADS]
    by_q = lambda a: a.reshape(bsz, nq, per_q, ATT_HEADS)
    qmax = by_q(qn).max(axis=2)[:, :, None, :]
    kdiag = by_q(kn).max(axis=2)[:, :, None, :]
    decay = c_last[:, None, :, :] - by_q(c_first)[:, :, 0, None, :]
    key_tile = jnp.arange(nt, dtype=jnp.int32)[None, None, :, None]
    before = key_tile < (jnp.arange(nq, dtype=jnp.int32) * per_q)[None, :, None, None]
    dead = (qmax * (kn[:, None, :, :] + kdiag) + SKIP_MARGIN < decay) & before
    lead = jnp.min(jnp.where(dead, nt, key_tile), axis=2)
    lead = lead.reshape(bsz, nq, ATT_HEADS // 2, 2).min(axis=-1)
    return jnp.transpose(lead, (0, 2, 1)).reshape(-1).astype(jnp.int32)


def _fox_attn(qa, ka, va, z_att, g2, skip):
    bsz, _, seqlen, _ = qa.shape
    nq = seqlen // ATT_TQ
    q_spec = pl.BlockSpec((1, 2, ATT_TQ, LANES), lambda b, p, i, s: (b, p, i, 0))
    kv_spec = pl.BlockSpec((1, 2, seqlen, LANES), lambda b, p, i, s: (b, p, 0, 0))
    row_spec = pl.BlockSpec((ATT_TQ, LANES), lambda b, p, i, s: (b * nq + i, p))
    scratch = [((2, ATT_TQ, ATT_TK), F32), ((2, ATT_TQ, LANES), F32), ((2, ATT_TQ, LANES), F32)]
    vmem = _vmem_limit(
        pipelined=[((2, ATT_TQ, LANES), BF16), ((2 * 2, seqlen, LANES), BF16),
                   ((2 * ATT_TQ, LANES), BF16), _shape_dtype(g2)],
        resident=scratch,
        temporaries=[((2 * ATT_TQ, ATT_TK), F32), ((2 * ATT_TQ, ATT_TK), BF16),
                     ((4 * ATT_TQ, LANES), F32)])
    return pl.pallas_call(
        _fox_attn_kernel,
        out_shape=jax.ShapeDtypeStruct((bsz * seqlen, ATT_WIDTH), BF16),
        grid_spec=pltpu.PrefetchScalarGridSpec(
            num_scalar_prefetch=1,
            grid=(bsz, ATT_HEADS // 2, nq),
            in_specs=[q_spec, kv_spec, kv_spec, row_spec,
                      pl.BlockSpec(g2.shape, lambda b, p, i, s: (0, 0))],
            out_specs=row_spec,
            scratch_shapes=[pltpu.VMEM(*a) for a in scratch]),
        compiler_params=pltpu.CompilerParams(
            dimension_semantics=("arbitrary", "arbitrary", "arbitrary"),
            vmem_limit_bytes=vmem),
        name="fox_attn",
    )(skip, qa, ka, va, z_att, g2)


def _rms(h, g):
    ms = jnp.mean(h * h, axis=-1, keepdims=True)
    return h * lax.rsqrt(ms + EPS) * g


def _out_ple_kernel(x_ref, ys_ref, ya_ref, p_ref, wo_s_ref, wo_a_ref, wg_ref, wp_ref,
                    gp_ref, gf_ref, o_ref):
    h = x_ref[...]
    h = h + (jnp.dot(ys_ref[...], wo_s_ref[...], preferred_element_type=F32)
             + jnp.dot(ya_ref[...], wo_a_ref[...], preferred_element_type=F32))
    hn = _rms(h, gp_ref[...]).astype(BF16)
    gate = jax.nn.sigmoid(jnp.dot(hn, wg_ref[...], preferred_element_type=F32))
    emb = jnp.dot(p_ref[...].astype(BF16), wp_ref[...], preferred_element_type=F32)
    h = h + gate * emb
    o_ref[...] = _rms(h, gf_ref[...])


def _out_ple(x2, y_ssd, y_att, p2, wo_s, wo_a, wg, wp, gp, gf):
    n = x2.shape[0]
    row = lambda w: pl.BlockSpec((OUT_ROWS, w), lambda i: (i, 0))
    full = _const_spec
    resident = [wo_s, wo_a, wg, wp, gp, gf]
    vmem = _vmem_limit(
        pipelined=[((2 * OUT_ROWS, D_MODEL), F32), ((2 * OUT_ROWS, D_MODEL), BF16),
                   ((OUT_ROWS, PLE_DIM), F32)],
        resident=[_shape_dtype(a) for a in resident],
        temporaries=[((3 * OUT_ROWS, D_MODEL), F32), ((OUT_ROWS, D_MODEL), BF16)])
    return pl.pallas_call(
        _out_ple_kernel,
        out_shape=jax.ShapeDtypeStruct((n, D_MODEL), F32),
        grid=(n // OUT_ROWS,),
        in_specs=[row(D_MODEL), row(SSD_WIDTH), row(ATT_WIDTH), row(PLE_DIM)]
        + [full(a) for a in resident],
        out_specs=row(D_MODEL),
        compiler_params=pltpu.CompilerParams(
            dimension_semantics=("arbitrary",), vmem_limit_bytes=vmem),
        name="out_ple",
    )(x2, y_ssd, y_att, p2, *resident)


def _layer(h2, p2, bsz, seqlen, norm_g, w_in, conv_w, conv_b, dt_bias, a_log, d_skip,
           ssd_norm_g, fg_bias, att_norm_g, w_out, ple_norm_g, w_ple_gate, w_ple_proj, out_g):
    o_zs = 0
    o_dt = o_zs + SSD_WIDTH + CONV_CH
    o_za = o_dt + SSD_HEADS
    o_q = o_za + ATT_WIDTH
    o_k = o_q + ATT_WIDTH
    o_v = o_k + ATT_WIDTH
    o_f = o_v + ATT_WIDTH
    w_in16 = w_in.astype(BF16)
    order = jnp.argsort(fg_bias)
    fg_bias = fg_bias[order]

    def heads_cols(w):
        return w.reshape(D_MODEL, ATT_HEADS, ATT_HEAD_DIM)[:, order].reshape(D_MODEL, ATT_WIDTH)

    w_secs = [w_in16[:, o_zs:o_dt]] + [
        heads_cols(w_in16[:, o:o + ATT_WIDTH]) for o in (o_za, o_q, o_k, o_v)]
    w_f = w_in16[:, o_f:o_f + ATT_HEADS][:, order]
    w_tail = jnp.concatenate(
        [w_in16[:, o_dt:o_za]] + [w_f] * SPLIT_PARTS
        + [jnp.zeros((D_MODEL, LANES - SSD_HEADS - SPLIT_PARTS * ATT_HEADS), BF16)], axis=1)
    w_out_att = w_out[SSD_WIDTH:].astype(BF16).reshape(
        ATT_HEADS, ATT_HEAD_DIM, D_MODEL)[order].reshape(ATT_WIDTH, D_MODEL)

    def lanes16(v, off):
        return jnp.zeros((1, LANES), F32).at[0, off:off + v.shape[0]].set(v.astype(F32))

    z_ssd, xbc, z_att, qa, ka, va, tail, stats = _in_proj(
        h2, norm_g.reshape(1, D_MODEL).astype(F32), w_secs, w_tail,
        lanes16(jnp.tile(fg_bias, SPLIT_PARTS), TAIL_F), bsz, seqlen)

    y_ssd = _ssd(
        xbc, tail, z_ssd, conv_w.astype(F32), conv_b.reshape(1, CONV_CH).astype(F32),
        lanes16(dt_bias, TAIL_DT), lanes16(-jnp.exp(a_log.astype(F32)), TAIL_DT),
        jnp.repeat(d_skip.astype(F32), SSD_HEAD_DIM).reshape(1, SSD_WIDTH),
        ssd_norm_g.reshape(1, SSD_WIDTH).astype(F32), bsz, seqlen)

    y_att = _fox_attn(qa, ka, va, z_att,
                      jnp.tile(att_norm_g.astype(F32), 2).reshape(1, LANES),
                      _skip_table(stats, bsz, seqlen))

    return _out_ple(
        h2, y_ssd, y_att, p2,
        w_out[:SSD_WIDTH].astype(BF16), w_out_att,
        w_ple_gate.astype(BF16), w_ple_proj.astype(BF16),
        ple_norm_g.reshape(1, D_MODEL).astype(F32), out_g.reshape(1, D_MODEL).astype(F32))


def kernel(x, p, norm_g, w_in, conv_w, conv_b, dt_bias, a_log, d_skip, ssd_norm_g, fg_bias,
           att_norm_g, w_out, ple_norm_g, w_ple_gate, w_ple_proj, final_norm_g):
    bsz, seqlen, _ = x.shape
    depth = p.shape[0]
    assert depth == 1, "the fused tail applies the final norm right after the only layer"
    h2 = x.reshape(bsz * seqlen, D_MODEL)
    out = _layer(h2, p.reshape(bsz * seqlen, PLE_DIM), bsz, seqlen,
                 norm_g[0], w_in[0], conv_w[0], conv_b[0], dt_bias[0], a_log[0], d_skip[0],
                 ssd_norm_g[0], fg_bias[0], att_norm_g[0], w_out[0], ple_norm_g[0],
                 w_ple_gate[0], w_ple_proj[0], final_norm_g)
    return out.reshape(bsz, seqlen, D_MODEL)
```

```python
import jax
import jax.numpy as jnp
import numpy as np
from jax import lax
from jax.experimental import pallas as pl
from jax.experimental.pallas import tpu as pltpu

F32 = jnp.float32
BF16 = jnp.bfloat16

D_MODEL = 1024
SSD_WIDTH = 1024
ATT_WIDTH = 1024
SSD_HEAD_DIM = 64
SSD_HEADS = 16
N_GROUPS = 2
HEADS_PER_GROUP = SSD_HEADS // N_GROUPS
GROUP_WIDTH = SSD_WIDTH // N_GROUPS
D_STATE = 128
CONV_WIDTH = 4
CHUNK = 128
ATT_HEAD_DIM = 64
ATT_HEADS = 16
PLE_DIM = 256
EPS = 1e-6
CONV_CH = SSD_WIDTH + 2 * N_GROUPS * D_STATE

LANES = 128
SUBLANES = 8
HALF = LANES // 2
BF16_ROWS = 16
CONV_HALO = 128
V7X_VMEM_BYTES = 64 * 1024 * 1024

LOG2E = 1.4426950408889634
NEG_BIG = -1e30
SKIP_MARGIN = float("inf")
NORM_SLACK = 1.01

SEC_ZS = (0, 0, SSD_WIDTH)
SEC_XBC = (0, SSD_WIDTH, CONV_CH)
SEC_ZA = (1, 0, ATT_WIDTH)
SEC_Q = (2, 0, ATT_WIDTH)
SEC_K = (3, 0, ATT_WIDTH)
SEC_V = (4, 0, ATT_WIDTH)
TAIL_DT = 0
TAIL_F = SSD_HEADS
SPLIT_PARTS = 3
TAIL_ONE = TAIL_DT

IN_ROWS = 512
IN_COL_CHUNK = 512
SSD_ROWS = 512
ATT_TQ = 1024
ATT_TK = 512
OUT_ROWS = 1024


def _nbytes(shape, dtype):
    return int(np.prod(shape)) * jnp.dtype(dtype).itemsize


def _vmem_limit(pipelined, resident, temporaries):
    need = (2 * sum(_nbytes(*a) for a in pipelined) + sum(_nbytes(*a) for a in resident)
            + sum(_nbytes(*a) for a in temporaries))
    assert need <= V7X_VMEM_BYTES, need
    return need


def _shape_dtype(a):
    return (a.shape, a.dtype)


def _split_bf16(x, parts):
    out = []
    r = x
    for i in range(parts):
        h = r.astype(BF16)
        out.append(h)
        if i + 1 < parts:
            r = r - h.astype(F32)
    return out


def _softplus(x):
    return jnp.maximum(x, 0.0) + jnp.log1p(jnp.exp(-jnp.abs(x)))


def _silu(x):
    return x * jax.nn.sigmoid(x)


def _aug_lane(head, j):
    pair, odd = divmod(head, 2)
    return pair * LANES + (j if odd else HALF + j)


def _aug_matrices():
    eq = np.zeros((LANES, ATT_WIDTH), np.float32)
    ek = np.zeros((LANES, ATT_WIDTH), np.float32)
    for h in range(ATT_HEADS):
        for part in range(SPLIT_PARTS):
            src = TAIL_F + part * ATT_HEADS + h
            eq[src, _aug_lane(h, part)] = 1.0
            ek[src, _aug_lane(h, SPLIT_PARTS + part)] = -1.0
            eq[TAIL_ONE, _aug_lane(h, SPLIT_PARTS + part)] = 1.0
            ek[TAIL_ONE, _aug_lane(h, part)] = 1.0
    return jnp.asarray(eq, BF16), jnp.asarray(ek, BF16)


def _tri3():
    tri = np.tril(np.ones((CHUNK, CHUNK), np.float32))
    return jnp.asarray(np.concatenate([tri, tri, tri], axis=1), BF16)


def _cumsum_rows(tri3, v):
    parts = _split_bf16(v, 3)
    return jnp.dot(tri3, jnp.concatenate(parts, axis=0), preferred_element_type=F32)


def _in_proj_kernel(x_ref, g_ref, ws_ref, wza_ref, wq_ref, wk_ref, wv_ref, wt_ref, fgb_ref,
                    tri3_ref, eq_ref, ek_ref,
                    zs_ref, xbc_ref, za_ref, qa_ref, ka_ref, va_ref, tail_ref, stats_ref,
                    carry_ref):
    w_refs = (ws_ref, wza_ref, wq_ref, wk_ref, wv_ref)

    @pl.when(pl.program_id(1) == 0)
    def _():
        carry_ref[...] = jnp.zeros_like(carry_ref)

    x = x_ref[...]
    ms = jnp.mean(x * x, axis=-1, keepdims=True)
    u = (x * lax.rsqrt(ms + EPS) * g_ref[...]).astype(BF16)

    def project(sec, c):
        return jnp.dot(u, w_refs[sec[0]][:, sec[1] + c:sec[1] + c + IN_COL_CHUNK],
                       preferred_element_type=F32)

    def section(out_ref, sec):
        for c in range(0, sec[2], IN_COL_CHUNK):
            out_ref[:, c:c + IN_COL_CHUNK] = project(sec, c).astype(out_ref.dtype)

    tail = jnp.dot(u, wt_ref[...], preferred_element_type=F32)
    tail_ref[...] = tail

    tri3 = tri3_ref[...]
    part_lane = lax.broadcasted_iota(jnp.int32, (CHUNK, LANES), 1)
    carry = carry_ref[...]
    lhs = []
    for c in range(IN_ROWS // CHUNK):
        log_f = -_softplus(-(tail[c * CHUNK:(c + 1) * CHUNK] + fgb_ref[...]))
        cum = _cumsum_rows(tri3, log_f) + carry
        carry = cum[CHUNK - 1:CHUNK, :]
        if c == 0:
            c_first = cum[0:1, :] * LOG2E
        hi, mid, lo = _split_bf16(cum * LOG2E, SPLIT_PARTS)
        parts = jnp.where(part_lane < TAIL_F + ATT_HEADS, hi,
                          jnp.where(part_lane < TAIL_F + 2 * ATT_HEADS, mid, lo))
        lhs.append(jnp.where(part_lane == TAIL_ONE, jnp.ones_like(parts), parts))
    carry_ref[...] = carry
    lhs = jnp.concatenate(lhs, axis=0)
    aq = jnp.dot(lhs, eq_ref[...], preferred_element_type=F32).astype(BF16)
    ak = jnp.dot(lhs, ek_ref[...], preferred_element_type=F32).astype(BF16)

    section(zs_ref, SEC_ZS)
    section(xbc_ref, SEC_XBC)
    section(za_ref, SEC_ZA)

    lane = lax.broadcasted_iota(jnp.int32, (IN_ROWS, LANES), 1)
    low = lane < HALF
    one_even = jnp.where(lane == HALF, 1.0, 0.0).astype(BF16)
    one_odd = jnp.where(lane == 0, 1.0, 0.0).astype(BF16)

    head_lane = lax.broadcasted_iota(jnp.int32, (1, LANES), 1)

    def head_major(out_ref, sec, aug_even, aug_odd, mul=None, norms=False):
        sq_max = jnp.zeros((1, LANES), F32)
        for c in range(0, sec[2], IN_COL_CHUNK):
            acc = project(sec, c)
            if mul is not None:
                acc = acc * mul
            acc16 = acc.astype(BF16)
            for i in range(IN_COL_CHUNK // LANES):
                pair = c // LANES + i
                d2 = acc16[:, i * LANES:(i + 1) * LANES]
                out_ref[0, 2 * pair] = jnp.where(low, d2, aug_even(pair))
                out_ref[0, 2 * pair + 1] = jnp.where(low, aug_odd(pair), d2)
                if norms:
                    sq = acc[:, i * LANES:(i + 1) * LANES]
                    sq = sq * sq
                    for odd in (0, 1):
                        mine = jnp.logical_not(low) if odd else low
                        rows = jnp.sum(jnp.where(mine, sq, 0.0), axis=-1, keepdims=True)
                        sq_max = jnp.where(head_lane == 2 * pair + odd,
                                           jnp.max(rows, axis=0, keepdims=True), sq_max)
        return sq_max

    pair_cols = lambda a: (lambda pair: a[:, pair * LANES:(pair + 1) * LANES])
    q_sq = head_major(qa_ref, SEC_Q, pair_cols(aq), pair_cols(aq),
                      mul=ATT_HEAD_DIM ** -0.5 * LOG2E, norms=True)
    k_sq = head_major(ka_ref, SEC_K, pair_cols(ak), pair_cols(ak), norms=True)
    head_major(va_ref, SEC_V, lambda pair: one_even, lambda pair: one_odd)
    stats_ref[0] = jnp.concatenate(
        [q_sq, k_sq, c_first, carry * LOG2E, jnp.zeros((SUBLANES - 4, LANES), F32)], axis=0)


def _const_spec(a):
    zeros = (0,) * a.ndim
    return pl.BlockSpec(a.shape, lambda *_: zeros, pipeline_mode=pl.Buffered(1))


def _in_proj(x2, g, w_secs, w_tail, fgb, bsz, seqlen):
    n = x2.shape[0]
    nt = seqlen // IN_ROWS
    tri3 = _tri3()
    eq, ek = _aug_matrices()
    row = lambda w: pl.BlockSpec((IN_ROWS, w), lambda b, t: (b * nt + t, 0))
    full = _const_spec
    head_major = pl.BlockSpec((1, ATT_HEADS, IN_ROWS, LANES), lambda b, t: (b, 0, t, 0))
    aug_shape = jax.ShapeDtypeStruct((bsz, ATT_HEADS, seqlen, LANES), BF16)
    out_shapes = (
        jax.ShapeDtypeStruct((n, SSD_WIDTH), BF16),
        jax.ShapeDtypeStruct((n, CONV_CH), BF16),
        jax.ShapeDtypeStruct((n, ATT_WIDTH), BF16),
        aug_shape, aug_shape, aug_shape,
        jax.ShapeDtypeStruct((n, LANES), F32),
        jax.ShapeDtypeStruct((bsz * nt, SUBLANES, LANES), F32),
    )
    resident = [g, *w_secs, w_tail, fgb, tri3, eq, ek]
    vmem = _vmem_limit(
        pipelined=[((IN_ROWS, D_MODEL), F32), ((IN_ROWS, SSD_WIDTH + CONV_CH + ATT_WIDTH), BF16),
                   ((3 * ATT_HEADS, IN_ROWS, LANES), BF16), ((IN_ROWS, LANES), F32)],
        resident=[_shape_dtype(a) for a in resident],
        temporaries=[((IN_ROWS, D_MODEL), BF16), ((2 * IN_ROWS, ATT_WIDTH), BF16),
                     ((8 * IN_ROWS, IN_COL_CHUNK), F32)])
    return pl.pallas_call(
        _in_proj_kernel,
        out_shape=out_shapes,
        grid=(bsz, nt),
        in_specs=[row(D_MODEL)] + [full(a) for a in resident],
        out_specs=(row(SSD_WIDTH), row(CONV_CH), row(ATT_WIDTH),
                   head_major, head_major, head_major, row(LANES),
                   pl.BlockSpec((1, SUBLANES, LANES), lambda b, t: (b * nt + t, 0, 0))),
        scratch_shapes=[pltpu.VMEM((1, LANES), F32)],
        compiler_params=pltpu.CompilerParams(
            dimension_semantics=("arbitrary", "arbitrary"), vmem_limit_bytes=vmem),
        name="in_proj",
    )(x2, *resident)


def _head_expand2():
    e = np.zeros((2 * LANES, SSD_WIDTH), np.float32)
    for h in range(SSD_HEADS):
        e[TAIL_DT + h, h * SSD_HEAD_DIM:(h + 1) * SSD_HEAD_DIM] = 1.0
        e[LANES + TAIL_DT + h, h * SSD_HEAD_DIM:(h + 1) * SSD_HEAD_DIM] = 1.0
    return jnp.asarray(e, BF16)


def _conv_shift():
    s = np.zeros((CONV_WIDTH * CHUNK, CONV_HALO + CHUNK), np.float32)
    for j in range(CONV_WIDTH):
        for l in range(CHUNK):
            s[j * CHUNK + l, CONV_HALO + l - (CONV_WIDTH - 1) + j] = 1.0
    return jnp.asarray(s, BF16)


def _ssd_kernel(xbc_ref, tail_ref, z_ref, cw_ref, cb_ref, dtb_ref, a_ref, dfull_ref, g_ref,
                tri3_ref, e2_ref, shift_ref, xprev_ref, y_ref, xwin_ref, state_ref):
    @pl.when(pl.program_id(1) == 0)
    def _():
        state_ref[...] = jnp.zeros_like(state_ref)

    prev = xprev_ref[...]
    xwin_ref[0:CONV_HALO, :] = jnp.where(pl.program_id(1) > 0, prev, jnp.zeros_like(prev))
    xwin_ref[CONV_HALO:CONV_HALO + SSD_ROWS, :] = xbc_ref[...]

    tri3 = tri3_ref[...]
    row_i = lax.broadcasted_iota(jnp.int32, (CHUNK, CHUNK), 0)
    col_i = lax.broadcasted_iota(jnp.int32, (CHUNK, CHUNK), 1)
    causal = row_i >= col_i
    low = col_i < HALF

    for c in range(SSD_ROWS // CHUNK):
        r0 = c * CHUNK
        window = xwin_ref[r0:r0 + CONV_HALO + CHUNK, :]
        taps = jnp.dot(shift_ref[...], window, preferred_element_type=F32)
        conv = cb_ref[...]
        for j in range(CONV_WIDTH):
            conv = conv + cw_ref[j:j + 1, :] * taps[j * CHUNK:(j + 1) * CHUNK]
        xc = _silu(conv)
        xs = xc[:, :SSD_WIDTH]
        bm = xc[:, SSD_WIDTH:SSD_WIDTH + N_GROUPS * D_STATE].astype(BF16)
        cm = xc[:, SSD_WIDTH + N_GROUPS * D_STATE:].astype(BF16)

        dt = _softplus(tail_ref[r0:r0 + CHUNK, :] + dtb_ref[...])
        adt = dt * a_ref[...]
        acs = _cumsum_rows(tri3, adt)
        acs_t = acs.T
        last = acs[CHUNK - 1:CHUNK, :]
        w = dt * jnp.exp(last - acs)
        ea = jnp.exp(acs)
        cd = jnp.broadcast_to(jnp.exp(last), (BF16_ROWS, LANES))
        stack = jnp.concatenate([dt, w, ea, cd], axis=0)
        lhs = jnp.concatenate(_split_bf16(stack, 2), axis=1)
        spread = jnp.dot(lhs, e2_ref[...], preferred_element_type=F32)
        dt_full = spread[0:CHUNK]
        w_full = spread[CHUNK:2 * CHUNK]
        ea_full = spread[2 * CHUNK:3 * CHUNK]
        cd_full = spread[3 * CHUNK:3 * CHUNK + 1]

        xdt = (xs * dt_full).astype(BF16)
        wx = (xs * w_full).astype(BF16)

        y_blocks = []
        yoff_blocks = []
        for g in range(N_GROUPS):
            gs = slice(g * D_STATE, (g + 1) * D_STATE)
            cb_g = lax.dot_general(cm[:, gs], bm[:, gs], (((1,), (1,)), ((), ())),
                                   preferred_element_type=F32)
            for jp in range(HEADS_PER_GROUP // 2):
                pair = g * (HEADS_PER_GROUP // 2) + jp
                x_pair = xdt[:, pair * LANES:(pair + 1) * LANES]
                ys = []
                for r in (2 * pair, 2 * pair + 1):
                    seg = acs[:, r:r + 1] - acs_t[r:r + 1, :]
                    dec = jnp.exp(jnp.where(causal, seg, -jnp.inf))
                    m = (cb_g * dec).astype(BF16)
                    ys.append(jnp.dot(m, x_pair, preferred_element_type=F32))
                y_blocks.append(jnp.where(low, ys[0], ys[1]))
            hs = slice(g * GROUP_WIDTH, (g + 1) * GROUP_WIDTH)
            st = state_ref[:, hs]
            yoff_blocks.append(jnp.dot(cm[:, gs], st.astype(BF16), preferred_element_type=F32))
            s_g = lax.dot_general(bm[:, gs], wx[:, hs], (((0,), (0,)), ((), ())),
                                  preferred_element_type=F32)
            state_ref[:, hs] = st * cd_full[:, hs] + s_g
        y = jnp.concatenate(y_blocks, axis=1)
        y_off = jnp.concatenate(yoff_blocks, axis=1)
        y = y + y_off * ea_full + dfull_ref[...] * xs
        y = y * _silu(z_ref[r0:r0 + CHUNK, :].astype(F32))
        outs = []
        for g in range(N_GROUPS):
            yg = y[:, g * GROUP_WIDTH:(g + 1) * GROUP_WIDTH]
            ms = jnp.mean(yg * yg, axis=-1, keepdims=True)
            outs.append(yg * lax.rsqrt(ms + EPS))
        yn = jnp.concatenate(outs, axis=1) * g_ref[...]
        y_ref[r0:r0 + CHUNK, :] = yn.astype(y_ref.dtype)


def _ssd(xbc, tail, z_ssd, conv_w, conv_b, dtb, a_neg, d_full, g, bsz, seqlen):
    nt = seqlen // SSD_ROWS
    tri3 = _tri3()
    e2 = _head_expand2()
    shift = _conv_shift()
    row = lambda w: pl.BlockSpec((SSD_ROWS, w), lambda b, t: (b * nt + t, 0))
    full = _const_spec
    halo_per_tile = SSD_ROWS // CONV_HALO
    prev = pl.BlockSpec(
        (CONV_HALO, CONV_CH),
        lambda b, t: (jnp.maximum((b * nt + t) * halo_per_tile - 1, 0), 0))
    resident = [conv_w, conv_b, dtb, a_neg, d_full, g, tri3, e2, shift]
    xwin = ((CONV_HALO + SSD_ROWS, CONV_CH), BF16)
    state = ((D_STATE, SSD_WIDTH), F32)
    vmem = _vmem_limit(
        pipelined=[((SSD_ROWS + CONV_HALO, CONV_CH), BF16), ((SSD_ROWS, LANES), F32),
                   ((2 * SSD_ROWS, SSD_WIDTH), BF16)],
        resident=[_shape_dtype(a) for a in resident] + [xwin, state],
        temporaries=[((CONV_WIDTH * CHUNK + 2 * CHUNK, CONV_CH), F32),
                     ((400 + 6 * CHUNK, SSD_WIDTH), F32)] * (SSD_ROWS // CHUNK))
    return pl.pallas_call(
        _ssd_kernel,
        out_shape=jax.ShapeDtypeStruct((bsz * seqlen, SSD_WIDTH), BF16),
        grid=(bsz, nt),
        in_specs=[row(CONV_CH), row(LANES), row(SSD_WIDTH)] + [full(a) for a in resident]
        + [prev],
        out_specs=row(SSD_WIDTH),
        scratch_shapes=[pltpu.VMEM(*xwin), pltpu.VMEM(*state)],
        compiler_params=pltpu.CompilerParams(
            dimension_semantics=("arbitrary", "arbitrary"), vmem_limit_bytes=vmem),
        name="ssd",
    )(xbc, tail, z_ssd, *resident, xbc)


def _fox_attn_kernel(skip_ref, qa_ref, ka_ref, va_ref, z_ref, g_ref, o_ref, s_ref, acc_ref,
                     m_ref):
    qi = pl.program_id(2)
    lane = lax.broadcasted_iota(jnp.int32, (ATT_TQ, LANES), 1)
    low = lane < HALF
    nt_dims = (((1,), (1,)), ((), ()))
    n_diag = ATT_TQ // ATT_TK
    all_rows = (0, ATT_TQ)

    def scores(hh, kstart, rows=all_rows):
        k = ka_ref[0, hh, pl.ds(kstart, ATT_TK), :]
        s_ref[hh, rows[0]:rows[1], :] = lax.dot_general(
            qa_ref[0, hh, rows[0]:rows[1], :], k, nt_dims, preferred_element_type=F32)

    def softmax_pv(hh, kstart, rows=all_rows, diag=None):
        r0, r1 = rows
        s = s_ref[hh, r0:r1, :]
        if diag is not None:
            row_i = lax.broadcasted_iota(jnp.int32, (r1 - r0, ATT_TK), 0) + r0
            col_i = lax.broadcasted_iota(jnp.int32, (r1 - r0, ATT_TK), 1) + diag * ATT_TK
            s = jnp.where(row_i >= col_i, s, -jnp.inf)
        m_old = m_ref[hh, r0:r1, :]
        m_new = jnp.maximum(m_old, jnp.max(s, axis=-1, keepdims=True))
        p = jnp.exp2(s - jnp.tile(m_new, (1, ATT_TK // LANES))).astype(BF16)
        alpha = jnp.exp2(m_old - m_new)
        m_ref[hh, r0:r1, :] = m_new
        v = va_ref[0, hh, pl.ds(kstart, ATT_TK), :]
        acc_ref[hh, r0:r1, :] = (alpha * acc_ref[hh, r0:r1, :]
                                 + jnp.dot(p, v, preferred_element_type=F32))

    first = skip_ref[(pl.program_id(0) * pl.num_programs(1) + pl.program_id(1))
                     * pl.num_programs(2) + qi]
    m_ref[...] = jnp.full(m_ref.shape, NEG_BIG, F32)
    acc_ref[...] = jnp.zeros(acc_ref.shape, F32)
    scores(0, pl.multiple_of(first * ATT_TK, ATT_TK))

    def key_tile_step(kstart):
        scores(1, kstart)
        softmax_pv(0, kstart)
        scores(0, kstart + ATT_TK)
        softmax_pv(1, kstart)

    @pl.when(first % 2 == 1)
    def _():
        key_tile_step(pl.multiple_of(first * ATT_TK, ATT_TK))

    def body(j, carry):
        for u in range(n_diag):
            key_tile_step(pl.multiple_of(j * ATT_TQ, ATT_TQ) + u * ATT_TK)
        return carry

    lax.fori_loop((first + 1) // 2, qi, body, 0)
    kdiag = pl.multiple_of(qi * ATT_TQ, ATT_TQ)
    for d in range(n_diag):
        rows = (d * ATT_TK, ATT_TQ)
        scores(1, kdiag + d * ATT_TK, rows)
        softmax_pv(0, kdiag + d * ATT_TK, rows, diag=d)
        if d + 1 < n_diag:
            scores(0, kdiag + (d + 1) * ATT_TK, ((d + 1) * ATT_TK, ATT_TQ))
        softmax_pv(1, kdiag + d * ATT_TK, rows, diag=d)

    def normed(hh):
        acc = acc_ref[hh]
        l_lane = 0 if hh else HALF
        mine = jnp.logical_not(low) if hh else low
        l = jnp.sum(jnp.where(lane == l_lane, acc, 0.0), axis=-1, keepdims=True)
        ssq = jnp.sum(jnp.where(mine, acc * acc, 0.0), axis=-1, keepdims=True)
        inv_l = 1.0 / l
        return acc * (inv_l * lax.rsqrt(ssq * (inv_l * inv_l) * (1.0 / ATT_HEAD_DIM) + EPS))

    att = jnp.where(low, normed(0), normed(1)) * g_ref[...]
    o_ref[...] = (att * _silu(z_ref[...].astype(F32))).astype(o_ref.dtype)


def _skip_table(stats, bsz, seqlen):
    assert ATT_TK == IN_ROWS and ATT_TQ == 2 * ATT_TK
    nt = seqlen // ATT_TK
    nq = seqlen // ATT_TQ
    per_q = ATT_TQ // ATT_TK
    st = stats.reshape(bsz, nt, SUBLANES, LANES)
    qn = jnp.sqrt(st[:, :, 0, :ATT_HEADS]) * NORM_SLACK
    kn = jnp.sqrt(st[:, :, 1, :ATT_HEADS]) * NORM_SLACK
    c_first = st[:, :, 2, TAIL_F:TAIL_F + ATT_HEADS]
    c_last = st[:, :, 3, TAIL_F:TAIL_F + ATT_HEADS]
    by_q = lambda a: a.reshape(bsz, nq, per_q, ATT_HEADS)
    qmax = by_q(qn).max(axis=2)[:, :, None, :]
    kdiag = by_q(kn).max(axis=2)[:, :, None, :]
    decay = c_last[:, None, :, :] - by_q(c_first)[:, :, 0, None, :]
    key_tile = jnp.arange(nt, dtype=jnp.int32)[None, None, :, None]
    before = key_tile < (jnp.arange(nq, dtype=jnp.int32) * per_q)[None, :, None, None]
    dead = (qmax * (kn[:, None, :, :] + kdiag) + SKIP_MARGIN < decay) & before
    lead = jnp.min(jnp.where(dead, nt, key_tile), axis=2)
    lead = lead.reshape(bsz, nq, ATT_HEADS // 2, 2).min(axis=-1)
    return jnp.transpose(lead, (0, 2, 1)).reshape(-1).astype(jnp.int32)


def _fox_attn(qa, ka, va, z_att, g2, skip):
    bsz, _, seqlen, _ = qa.shape
    nq = seqlen // ATT_TQ
    q_spec = pl.BlockSpec((1, 2, ATT_TQ, LANES), lambda b, p, i, s: (b, p, i, 0))
    kv_spec = pl.BlockSpec((1, 2, seqlen, LANES), lambda b, p, i, s: (b, p, 0, 0))
    row_spec = pl.BlockSpec((ATT_TQ, LANES), lambda b, p, i, s: (b * nq + i, p))
    scratch = [((2, ATT_TQ, ATT_TK), F32), ((2, ATT_TQ, LANES), F32), ((2, ATT_TQ, LANES), F32)]
    vmem = _vmem_limit(
        pipelined=[((2, ATT_TQ, LANES), BF16), ((2 * 2, seqlen, LANES), BF16),
                   ((2 * ATT_TQ, LANES), BF16), _shape_dtype(g2)],
        resident=scratch,
        temporaries=[((2 * ATT_TQ, ATT_TK), F32), ((2 * ATT_TQ, ATT_TK), BF16),
                     ((4 * ATT_TQ, LANES), F32)])
    return pl.pallas_call(
        _fox_attn_kernel,
        out_shape=jax.ShapeDtypeStruct((bsz * seqlen, ATT_WIDTH), BF16),
        grid_spec=pltpu.PrefetchScalarGridSpec(
            num_scalar_prefetch=1,
            grid=(bsz, ATT_HEADS // 2, nq),
            in_specs=[q_spec, kv_spec, kv_spec, row_spec,
                      pl.BlockSpec(g2.shape, lambda b, p, i, s: (0, 0))],
            out_specs=row_spec,
            scratch_shapes=[pltpu.VMEM(*a) for a in scratch]),
        compiler_params=pltpu.CompilerParams(
            dimension_semantics=("arbitrary", "arbitrary", "arbitrary"),
            vmem_limit_bytes=vmem),
        name="fox_attn",
    )(skip, qa, ka, va, z_att, g2)


def _rms(h, g):
    ms = jnp.mean(h * h, axis=-1, keepdims=True)
    return h * lax.rsqrt(ms + EPS) * g


def _out_ple_kernel(x_ref, ys_ref, ya_ref, p_ref, wo_s_ref, wo_a_ref, wg_ref, wp_ref,
                    gp_ref, gf_ref, o_ref):
    h = x_ref[...]
    h = h + (jnp.dot(ys_ref[...], wo_s_ref[...], preferred_element_type=F32)
             + jnp.dot(ya_ref[...], wo_a_ref[...], preferred_element_type=F32))
    hn = _rms(h, gp_ref[...]).astype(BF16)
    gate = jax.nn.sigmoid(jnp.dot(hn, wg_ref[...], preferred_element_type=F32))
    emb = jnp.dot(p_ref[...].astype(BF16), wp_ref[...], preferred_element_type=F32)
    h = h + gate * emb
    o_ref[...] = _rms(h, gf_ref[...])


def _out_ple(x2, y_ssd, y_att, p2, wo_s, wo_a, wg, wp, gp, gf):
    n = x2.shape[0]
    row = lambda w: pl.BlockSpec((OUT_ROWS, w), lambda i: (i, 0))
    full = _const_spec
    resident = [wo_s, wo_a, wg, wp, gp, gf]
    vmem = _vmem_limit(
        pipelined=[((2 * OUT_ROWS, D_MODEL), F32), ((2 * OUT_ROWS, D_MODEL), BF16),
                   ((OUT_ROWS, PLE_DIM), F32)],
        resident=[_shape_dtype(a) for a in resident],
        temporaries=[((3 * OUT_ROWS, D_MODEL), F32), ((OUT_ROWS, D_MODEL), BF16)])
    return pl.pallas_call(
        _out_ple_kernel,
        out_shape=jax.ShapeDtypeStruct((n, D_MODEL), F32),
        grid=(n // OUT_ROWS,),
        in_specs=[row(D_MODEL), row(SSD_WIDTH), row(ATT_WIDTH), row(PLE_DIM)]
        + [full(a) for a in resident],
        out_specs=row(D_MODEL),
        compiler_params=pltpu.CompilerParams(
            dimension_semantics=("arbitrary",), vmem_limit_bytes=vmem),
        name="out_ple",
    )(x2, y_ssd, y_att, p2, *resident)


def _layer(h2, p2, bsz, seqlen, norm_g, w_in, conv_w, conv_b, dt_bias, a_log, d_skip,
           ssd_norm_g, fg_bias, att_norm_g, w_out, ple_norm_g, w_ple_gate, w_ple_proj, out_g):
    o_zs = 0
    o_dt = o_zs + SSD_WIDTH + CONV_CH
    o_za = o_dt + SSD_HEADS
    o_q = o_za + ATT_WIDTH
    o_k = o_q + ATT_WIDTH
    o_v = o_k + ATT_WIDTH
    o_f = o_v + ATT_WIDTH
    w_in16 = w_in.astype(BF16)
    order = jnp.argsort(fg_bias)
    fg_bias = fg_bias[order]

    def heads_cols(w):
        return w.reshape(D_MODEL, ATT_HEADS, ATT_HEAD_DIM)[:, order].reshape(D_MODEL, ATT_WIDTH)

    w_secs = [w_in16[:, o_zs:o_dt]] + [
        heads_cols(w_in16[:, o:o + ATT_WIDTH]) for o in (o_za, o_q, o_k, o_v)]
    w_f = w_in16[:, o_f:o_f + ATT_HEADS][:, order]
    w_tail = jnp.concatenate(
        [w_in16[:, o_dt:o_za]] + [w_f] * SPLIT_PARTS
        + [jnp.zeros((D_MODEL, LANES - SSD_HEADS - SPLIT_PARTS * ATT_HEADS), BF16)], axis=1)
    w_out_att = w_out[SSD_WIDTH:].astype(BF16).reshape(
        ATT_HEADS, ATT_HEAD_DIM, D_MODEL)[order].reshape(ATT_WIDTH, D_MODEL)

    def lanes16(v, off):
        return jnp.zeros((1, LANES), F32).at[0, off:off + v.shape[0]].set(v.astype(F32))

    z_ssd, xbc, z_att, qa, ka, va, tail, stats = _in_proj(
        h2, norm_g.reshape(1, D_MODEL).astype(F32), w_secs, w_tail,
        lanes16(jnp.tile(fg_bias, SPLIT_PARTS), TAIL_F), bsz, seqlen)

    y_ssd = _ssd(
        xbc, tail, z_ssd, conv_w.astype(F32), conv_b.reshape(1, CONV_CH).astype(F32),
        lanes16(dt_bias, TAIL_DT), lanes16(-jnp.exp(a_log.astype(F32)), TAIL_DT),
        jnp.repeat(d_skip.astype(F32), SSD_HEAD_DIM).reshape(1, SSD_WIDTH),
        ssd_norm_g.reshape(1, SSD_WIDTH).astype(F32), bsz, seqlen)

    y_att = _fox_attn(qa, ka, va, z_att,
                      jnp.tile(att_norm_g.astype(F32), 2).reshape(1, LANES),
                      _skip_table(stats, bsz, seqlen))

    return _out_ple(
        h2, y_ssd, y_att, p2,
        w_out[:SSD_WIDTH].astype(BF16), w_out_att,
        w_ple_gate.astype(BF16), w_ple_proj.astype(BF16),
        ple_norm_g.reshape(1, D_MODEL).astype(F32), out_g.reshape(1, D_MODEL).astype(F32))


def kernel(x, p, norm_g, w_in, conv_w, conv_b, dt_bias, a_log, d_skip, ssd_norm_g, fg_bias,
           att_norm_g, w_out, ple_norm_g, w_ple_gate, w_ple_proj, final_norm_g):
    bsz, seqlen, _ = x.shape
    depth = p.shape[0]
    assert depth == 1, "the fused tail applies the final norm right after the only layer"
    h2 = x.reshape(bsz * seqlen, D_MODEL)
    out = _layer(h2, p.reshape(bsz * seqlen, PLE_DIM), bsz, seqlen,
                 norm_g[0], w_in[0], conv_w[0], conv_b[0], dt_bias[0], a_log[0], d_skip[0],
                 ssd_norm_g[0], fg_bias[0], att_norm_g[0], w_out[0], ple_norm_g[0],
                 w_ple_gate[0], w_ple_proj[0], final_norm_g)
    return out.reshape(bsz, seqlen, D_MODEL)
```

```python
import jax
import jax.numpy as jnp
import numpy as np
from jax import lax
from jax.experimental import pallas as pl
from jax.experimental.pallas import tpu as pltpu

F32 = jnp.float32
BF16 = jnp.bfloat16

D_MODEL = 1024
SSD_WIDTH = 1024
ATT_WIDTH = 1024
SSD_HEAD_DIM = 64
SSD_HEADS = 16
N_GROUPS = 2
HEADS_PER_GROUP = SSD_HEADS // N_GROUPS
GROUP_WIDTH = SSD_WIDTH // N_GROUPS
D_STATE = 128
CONV_WIDTH = 4
CHUNK = 128
ATT_HEAD_DIM = 64
ATT_HEADS = 16
PLE_DIM = 256
EPS = 1e-6
CONV_CH = SSD_WIDTH + 2 * N_GROUPS * D_STATE

LANES = 128
SUBLANES = 8
HALF = LANES // 2
BF16_ROWS = 16
CONV_HALO = 128
V7X_VMEM_BYTES = 64 * 1024 * 1024

LOG2E = 1.4426950408889634
NEG_BIG = -1e30
SKIP_MARGIN = 136.0
NORM_SLACK = 1.01

SEC_ZS = (0, 0, SSD_WIDTH)
SEC_XBC = (0, SSD_WIDTH, CONV_CH)
SEC_ZA = (1, 0, ATT_WIDTH)
SEC_Q = (2, 0, ATT_WIDTH)
SEC_K = (3, 0, ATT_WIDTH)
SEC_V = (4, 0, ATT_WIDTH)
TAIL_DT = 0
TAIL_F = SSD_HEADS
SPLIT_PARTS = 3
TAIL_ONE = TAIL_DT

IN_ROWS = 512
IN_COL_CHUNK = 512
SSD_ROWS = 512
ATT_TQ = 1024
ATT_TK = 512
OUT_ROWS = 1024


def _nbytes(shape, dtype):
    return int(np.prod(shape)) * jnp.dtype(dtype).itemsize


def _vmem_limit(pipelined, resident, temporaries):
    need = (2 * sum(_nbytes(*a) for a in pipelined) + sum(_nbytes(*a) for a in resident)
            + sum(_nbytes(*a) for a in temporaries))
    assert need <= V7X_VMEM_BYTES, need
    return need


def _shape_dtype(a):
    return (a.shape, a.dtype)


def _split_bf16(x, parts):
    out = []
    r = x
    for i in range(parts):
        h = r.astype(BF16)
        out.append(h)
        if i + 1 < parts:
            r = r - h.astype(F32)
    return out


def _softplus(x):
    return jnp.maximum(x, 0.0) + jnp.log1p(jnp.exp(-jnp.abs(x)))


def _silu(x):
    return x * jax.nn.sigmoid(x)


def _aug_lane(head, j):
    pair, odd = divmod(head, 2)
    return pair * LANES + (j if odd else HALF + j)


def _aug_matrices():
    eq = np.zeros((LANES, ATT_WIDTH), np.float32)
    ek = np.zeros((LANES, ATT_WIDTH), np.float32)
    for h in range(ATT_HEADS):
        for part in range(SPLIT_PARTS):
            src = TAIL_F + part * ATT_HEADS + h
            eq[src, _aug_lane(h, part)] = 1.0
            ek[src, _aug_lane(h, SPLIT_PARTS + part)] = -1.0
            eq[TAIL_ONE, _aug_lane(h, SPLIT_PARTS + part)] = 1.0
            ek[TAIL_ONE, _aug_lane(h, part)] = 1.0
    return jnp.asarray(eq, BF16), jnp.asarray(ek, BF16)


def _tri3():
    tri = np.tril(np.ones((CHUNK, CHUNK), np.float32))
    return jnp.asarray(np.concatenate([tri, tri, tri], axis=1), BF16)


def _cumsum_rows(tri3, v):
    parts = _split_bf16(v, 3)
    return jnp.dot(tri3, jnp.concatenate(parts, axis=0), preferred_element_type=F32)


def _in_proj_kernel(x_ref, g_ref, ws_ref, wza_ref, wq_ref, wk_ref, wv_ref, wt_ref, fgb_ref,
                    tri3_ref, eq_ref, ek_ref,
                    zs_ref, xbc_ref, za_ref, qa_ref, ka_ref, va_ref, tail_ref, stats_ref,
                    carry_ref):
    w_refs = (ws_ref, wza_ref, wq_ref, wk_ref, wv_ref)

    @pl.when(pl.program_id(1) == 0)
    def _():
        carry_ref[...] = jnp.zeros_like(carry_ref)

    x = x_ref[...]
    ms = jnp.mean(x * x, axis=-1, keepdims=True)
    u = (x * lax.rsqrt(ms + EPS) * g_ref[...]).astype(BF16)

    def project(sec, c):
        return jnp.dot(u, w_refs[sec[0]][:, sec[1] + c:sec[1] + c + IN_COL_CHUNK],
                       preferred_element_type=F32)

    def section(out_ref, sec):
        for c in range(0, sec[2], IN_COL_CHUNK):
            out_ref[:, c:c + IN_COL_CHUNK] = project(sec, c).astype(out_ref.dtype)

    tail = jnp.dot(u, wt_ref[...], preferred_element_type=F32)
    tail_ref[...] = tail

    tri3 = tri3_ref[...]
    part_lane = lax.broadcasted_iota(jnp.int32, (CHUNK, LANES), 1)
    carry = carry_ref[...]
    lhs = []
    for c in range(IN_ROWS // CHUNK):
        log_f = -_softplus(-(tail[c * CHUNK:(c + 1) * CHUNK] + fgb_ref[...]))
        cum = _cumsum_rows(tri3, log_f) + carry
        carry = cum[CHUNK - 1:CHUNK, :]
        if c == 0:
            c_first = cum[0:1, :] * LOG2E
        hi, mid, lo = _split_bf16(cum * LOG2E, SPLIT_PARTS)
        parts = jnp.where(part_lane < TAIL_F + ATT_HEADS, hi,
                          jnp.where(part_lane < TAIL_F + 2 * ATT_HEADS, mid, lo))
        lhs.append(jnp.where(part_lane == TAIL_ONE, jnp.ones_like(parts), parts))
    carry_ref[...] = carry
    lhs = jnp.concatenate(lhs, axis=0)
    aq = jnp.dot(lhs, eq_ref[...], preferred_element_type=F32).astype(BF16)
    ak = jnp.dot(lhs, ek_ref[...], preferred_element_type=F32).astype(BF16)

    section(zs_ref, SEC_ZS)
    section(xbc_ref, SEC_XBC)
    section(za_ref, SEC_ZA)

    lane = lax.broadcasted_iota(jnp.int32, (IN_ROWS, LANES), 1)
    low = lane < HALF
    one_even = jnp.where(lane == HALF, 1.0, 0.0).astype(BF16)
    one_odd = jnp.where(lane == 0, 1.0, 0.0).astype(BF16)

    head_lane = lax.broadcasted_iota(jnp.int32, (1, LANES), 1)

    def head_major(out_ref, sec, aug_even, aug_odd, mul=None, norms=False):
        sq_max = jnp.zeros((1, LANES), F32)
        for c in range(0, sec[2], IN_COL_CHUNK):
            acc = project(sec, c)
            if mul is not None:
                acc = acc * mul
            acc16 = acc.astype(BF16)
            for i in range(IN_COL_CHUNK // LANES):
                pair = c // LANES + i
                d2 = acc16[:, i * LANES:(i + 1) * LANES]
                out_ref[0, 2 * pair] = jnp.where(low, d2, aug_even(pair))
                out_ref[0, 2 * pair + 1] = jnp.where(low, aug_odd(pair), d2)
                if norms:
                    sq = acc[:, i * LANES:(i + 1) * LANES]
                    sq = sq * sq
                    for odd in (0, 1):
                        mine = jnp.logical_not(low) if odd else low
                        rows = jnp.sum(jnp.where(mine, sq, 0.0), axis=-1, keepdims=True)
                        sq_max = jnp.where(head_lane == 2 * pair + odd,
                                           jnp.max(rows, axis=0, keepdims=True), sq_max)
        return sq_max

    pair_cols = lambda a: (lambda pair: a[:, pair * LANES:(pair + 1) * LANES])
    q_sq = head_major(qa_ref, SEC_Q, pair_cols(aq), pair_cols(aq),
                      mul=ATT_HEAD_DIM ** -0.5 * LOG2E, norms=True)
    k_sq = head_major(ka_ref, SEC_K, pair_cols(ak), pair_cols(ak), norms=True)
    head_major(va_ref, SEC_V, lambda pair: one_even, lambda pair: one_odd)
    stats_ref[0] = jnp.concatenate(
        [q_sq, k_sq, c_first, carry * LOG2E, jnp.zeros((SUBLANES - 4, LANES), F32)], axis=0)


def _const_spec(a):
    zeros = (0,) * a.ndim
    return pl.BlockSpec(a.shape, lambda *_: zeros, pipeline_mode=pl.Buffered(1))


def _in_proj(x2, g, w_secs, w_tail, fgb, bsz, seqlen):
    n = x2.shape[0]
    nt = seqlen // IN_ROWS
    tri3 = _tri3()
    eq, ek = _aug_matrices()
    row = lambda w: pl.BlockSpec((IN_ROWS, w), lambda b, t: (b * nt + t, 0))
    full = _const_spec
    head_major = pl.BlockSpec((1, ATT_HEADS, IN_ROWS, LANES), lambda b, t: (b, 0, t, 0))
    aug_shape = jax.ShapeDtypeStruct((bsz, ATT_HEADS, seqlen, LANES), BF16)
    out_shapes = (
        jax.ShapeDtypeStruct((n, SSD_WIDTH), BF16),
        jax.ShapeDtypeStruct((n, CONV_CH), BF16),
        jax.ShapeDtypeStruct((n, ATT_WIDTH), BF16),
        aug_shape, aug_shape, aug_shape,
        jax.ShapeDtypeStruct((n, LANES), F32),
        jax.ShapeDtypeStruct((bsz * nt, SUBLANES, LANES), F32),
    )
    resident = [g, *w_secs, w_tail, fgb, tri3, eq, ek]
    vmem = _vmem_limit(
        pipelined=[((IN_ROWS, D_MODEL), F32), ((IN_ROWS, SSD_WIDTH + CONV_CH + ATT_WIDTH), BF16),
                   ((3 * ATT_HEADS, IN_ROWS, LANES), BF16), ((IN_ROWS, LANES), F32)],
        resident=[_shape_dtype(a) for a in resident],
        temporaries=[((IN_ROWS, D_MODEL), BF16), ((2 * IN_ROWS, ATT_WIDTH), BF16),
                     ((8 * IN_ROWS, IN_COL_CHUNK), F32)])
    return pl.pallas_call(
        _in_proj_kernel,
        out_shape=out_shapes,
        grid=(bsz, nt),
        in_specs=[row(D_MODEL)] + [full(a) for a in resident],
        out_specs=(row(SSD_WIDTH), row(CONV_CH), row(ATT_WIDTH),
                   head_major, head_major, head_major, row(LANES),
                   pl.BlockSpec((1, SUBLANES, LANES), lambda b, t: (b * nt + t, 0, 0))),
        scratch_shapes=[pltpu.VMEM((1, LANES), F32)],
        compiler_params=pltpu.CompilerParams(
            dimension_semantics=("arbitrary", "arbitrary"), vmem_limit_bytes=vmem),
        name="in_proj",
    )(x2, *resident)


def _head_expand2():
    e = np.zeros((2 * LANES, SSD_WIDTH), np.float32)
    for h in range(SSD_HEADS):
        e[TAIL_DT + h, h * SSD_HEAD_DIM:(h + 1) * SSD_HEAD_DIM] = 1.0
        e[LANES + TAIL_DT + h, h * SSD_HEAD_DIM:(h + 1) * SSD_HEAD_DIM] = 1.0
    return jnp.asarray(e, BF16)


def _conv_shift():
    s = np.zeros((CONV_WIDTH * CHUNK, CONV_HALO + CHUNK), np.float32)
    for j in range(CONV_WIDTH):
        for l in range(CHUNK):
            s[j * CHUNK + l, CONV_HALO + l - (CONV_WIDTH - 1) + j] = 1.0
    return jnp.asarray(s, BF16)


def _ssd_kernel(xbc_ref, tail_ref, z_ref, cw_ref, cb_ref, dtb_ref, a_ref, dfull_ref, g_ref,
                tri3_ref, e2_ref, shift_ref, xprev_ref, y_ref, xwin_ref, state_ref):
    @pl.when(pl.program_id(1) == 0)
    def _():
        state_ref[...] = jnp.zeros_like(state_ref)

    prev = xprev_ref[...]
    xwin_ref[0:CONV_HALO, :] = jnp.where(pl.program_id(1) > 0, prev, jnp.zeros_like(prev))
    xwin_ref[CONV_HALO:CONV_HALO + SSD_ROWS, :] = xbc_ref[...]

    tri3 = tri3_ref[...]
    row_i = lax.broadcasted_iota(jnp.int32, (CHUNK, CHUNK), 0)
    col_i = lax.broadcasted_iota(jnp.int32, (CHUNK, CHUNK), 1)
    causal = row_i >= col_i
    low = col_i < HALF

    for c in range(SSD_ROWS // CHUNK):
        r0 = c * CHUNK
        window = xwin_ref[r0:r0 + CONV_HALO + CHUNK, :]
        taps = jnp.dot(shift_ref[...], window, preferred_element_type=F32)
        conv = cb_ref[...]
        for j in range(CONV_WIDTH):
            conv = conv + cw_ref[j:j + 1, :] * taps[j * CHUNK:(j + 1) * CHUNK]
        xc = _silu(conv)
        xs = xc[:, :SSD_WIDTH]
        bm = xc[:, SSD_WIDTH:SSD_WIDTH + N_GROUPS * D_STATE].astype(BF16)
        cm = xc[:, SSD_WIDTH + N_GROUPS * D_STATE:].astype(BF16)

        dt = _softplus(tail_ref[r0:r0 + CHUNK, :] + dtb_ref[...])
        adt = dt * a_ref[...]
        acs = _cumsum_rows(tri3, adt)
        acs_t = acs.T
        last = acs[CHUNK - 1:CHUNK, :]
        w = dt * jnp.exp(last - acs)
        ea = jnp.exp(acs)
        cd = jnp.broadcast_to(jnp.exp(last), (BF16_ROWS, LANES))
        stack = jnp.concatenate([dt, w, ea, cd], axis=0)
        lhs = jnp.concatenate(_split_bf16(stack, 2), axis=1)
        spread = jnp.dot(lhs, e2_ref[...], preferred_element_type=F32)
        dt_full = spread[0:CHUNK]
        w_full = spread[CHUNK:2 * CHUNK]
        ea_full = spread[2 * CHUNK:3 * CHUNK]
        cd_full = spread[3 * CHUNK:3 * CHUNK + 1]

        xdt = (xs * dt_full).astype(BF16)
        wx = (xs * w_full).astype(BF16)

        y_blocks = []
        yoff_blocks = []
        for g in range(N_GROUPS):
            gs = slice(g * D_STATE, (g + 1) * D_STATE)
            cb_g = lax.dot_general(cm[:, gs], bm[:, gs], (((1,), (1,)), ((), ())),
                                   preferred_element_type=F32)
            for jp in range(HEADS_PER_GROUP // 2):
                pair = g * (HEADS_PER_GROUP // 2) + jp
                x_pair = xdt[:, pair * LANES:(pair + 1) * LANES]
                ys = []
                for r in (2 * pair, 2 * pair + 1):
                    seg = acs[:, r:r + 1] - acs_t[r:r + 1, :]
                    dec = jnp.exp(jnp.where(causal, seg, -jnp.inf))
                    m = (cb_g * dec).astype(BF16)
                    ys.append(jnp.dot(m, x_pair, preferred_element_type=F32))
                y_blocks.append(jnp.where(low, ys[0], ys[1]))
            hs = slice(g * GROUP_WIDTH, (g + 1) * GROUP_WIDTH)
            st = state_ref[:, hs]
            yoff_blocks.append(jnp.dot(cm[:, gs], st.astype(BF16), preferred_element_type=F32))
            s_g = lax.dot_general(bm[:, gs], wx[:, hs], (((0,), (0,)), ((), ())),
                                  preferred_element_type=F32)
            state_ref[:, hs] = st * cd_full[:, hs] + s_g
        y = jnp.concatenate(y_blocks, axis=1)
        y_off = jnp.concatenate(yoff_blocks, axis=1)
        y = y + y_off * ea_full + dfull_ref[...] * xs
        y = y * _silu(z_ref[r0:r0 + CHUNK, :].astype(F32))
        outs = []
        for g in range(N_GROUPS):
            yg = y[:, g * GROUP_WIDTH:(g + 1) * GROUP_WIDTH]
            ms = jnp.mean(yg * yg, axis=-1, keepdims=True)
            outs.append(yg * lax.rsqrt(ms + EPS))
        yn = jnp.concatenate(outs, axis=1) * g_ref[...]
        y_ref[r0:r0 + CHUNK, :] = yn.astype(y_ref.dtype)


def _ssd(xbc, tail, z_ssd, conv_w, conv_b, dtb, a_neg, d_full, g, bsz, seqlen):
    nt = seqlen // SSD_ROWS
    tri3 = _tri3()
    e2 = _head_expand2()
    shift = _conv_shift()
    row = lambda w: pl.BlockSpec((SSD_ROWS, w), lambda b, t: (b * nt + t, 0))
    full = _const_spec
    halo_per_tile = SSD_ROWS // CONV_HALO
    prev = pl.BlockSpec(
        (CONV_HALO, CONV_CH),
        lambda b, t: (jnp.maximum((b * nt + t) * halo_per_tile - 1, 0), 0))
    resident = [conv_w, conv_b, dtb, a_neg, d_full, g, tri3, e2, shift]
    xwin = ((CONV_HALO + SSD_ROWS, CONV_CH), BF16)
    state = ((D_STATE, SSD_WIDTH), F32)
    vmem = _vmem_limit(
        pipelined=[((SSD_ROWS + CONV_HALO, CONV_CH), BF16), ((SSD_ROWS, LANES), F32),
                   ((2 * SSD_ROWS, SSD_WIDTH), BF16)],
        resident=[_shape_dtype(a) for a in resident] + [xwin, state],
        temporaries=[((CONV_WIDTH * CHUNK + 2 * CHUNK, CONV_CH), F32),
                     ((400 + 6 * CHUNK, SSD_WIDTH), F32)] * (SSD_ROWS // CHUNK))
    return pl.pallas_call(
        _ssd_kernel,
        out_shape=jax.ShapeDtypeStruct((bsz * seqlen, SSD_WIDTH), BF16),
        grid=(bsz, nt),
        in_specs=[row(CONV_CH), row(LANES), row(SSD_WIDTH)] + [full(a) for a in resident]
        + [prev],
        out_specs=row(SSD_WIDTH),
        scratch_shapes=[pltpu.VMEM(*xwin), pltpu.VMEM(*state)],
        compiler_params=pltpu.CompilerParams(
            dimension_semantics=("arbitrary", "arbitrary"), vmem_limit_bytes=vmem),
        name="ssd",
    )(xbc, tail, z_ssd, *resident, xbc)


def _fox_attn_kernel(skip_ref, qa_ref, ka_ref, va_ref, z_ref, g_ref, o_ref, s_ref, acc_ref,
                     m_ref):
    qi = pl.program_id(2)
    lane = lax.broadcasted_iota(jnp.int32, (ATT_TQ, LANES), 1)
    low = lane < HALF
    nt_dims = (((1,), (1,)), ((), ()))
    n_diag = ATT_TQ // ATT_TK
    all_rows = (0, ATT_TQ)

    def scores(hh, kstart, rows=all_rows):
        k = ka_ref[0, hh, pl.ds(kstart, ATT_TK), :]
        s_ref[hh, rows[0]:rows[1], :] = lax.dot_general(
            qa_ref[0, hh, rows[0]:rows[1], :], k, nt_dims, preferred_element_type=F32)

    def softmax_pv(hh, kstart, rows=all_rows, diag=None):
        r0, r1 = rows
        s = s_ref[hh, r0:r1, :]
        if diag is not None:
            row_i = lax.broadcasted_iota(jnp.int32, (r1 - r0, ATT_TK), 0) + r0
            col_i = lax.broadcasted_iota(jnp.int32, (r1 - r0, ATT_TK), 1) + diag * ATT_TK
            s = jnp.where(row_i >= col_i, s, -jnp.inf)
        m_old = m_ref[hh, r0:r1, :]
        m_new = jnp.maximum(m_old, jnp.max(s, axis=-1, keepdims=True))
        p = jnp.exp2(s - jnp.tile(m_new, (1, ATT_TK // LANES))).astype(BF16)
        alpha = jnp.exp2(m_old - m_new)
        m_ref[hh, r0:r1, :] = m_new
        v = va_ref[0, hh, pl.ds(kstart, ATT_TK), :]
        acc_ref[hh, r0:r1, :] = (alpha * acc_ref[hh, r0:r1, :]
                                 + jnp.dot(p, v, preferred_element_type=F32))

    first = skip_ref[(pl.program_id(0) * pl.num_programs(1) + pl.program_id(1))
                     * pl.num_programs(2) + qi]
    m_ref[...] = jnp.full(m_ref.shape, NEG_BIG, F32)
    acc_ref[...] = jnp.zeros(acc_ref.shape, F32)
    scores(0, pl.multiple_of(first * ATT_TK, ATT_TK))

    def key_tile_step(kstart):
        scores(1, kstart)
        softmax_pv(0, kstart)
        scores(0, kstart + ATT_TK)
        softmax_pv(1, kstart)

    @pl.when(first % 2 == 1)
    def _():
        key_tile_step(pl.multiple_of(first * ATT_TK, ATT_TK))

    def two_tiles(trip):
        for u in range(n_diag):
            key_tile_step(pl.multiple_of(trip * ATT_TQ, ATT_TQ) + u * ATT_TK)

    trip0 = (first + 1) // 2
    n_trips = qi - trip0

    @pl.when(n_trips % 2 == 1)
    def _():
        two_tiles(trip0)

    trip1 = trip0 + n_trips % 2

    def body(j, carry):
        two_tiles(trip1 + 2 * j)
        two_tiles(trip1 + 2 * j + 1)
        return carry

    lax.fori_loop(0, n_trips // 2, body, 0)
    kdiag = pl.multiple_of(qi * ATT_TQ, ATT_TQ)
    for d in range(n_diag):
        rows = (d * ATT_TK, ATT_TQ)
        scores(1, kdiag + d * ATT_TK, rows)
        softmax_pv(0, kdiag + d * ATT_TK, rows, diag=d)
        if d + 1 < n_diag:
            scores(0, kdiag + (d + 1) * ATT_TK, ((d + 1) * ATT_TK, ATT_TQ))
        softmax_pv(1, kdiag + d * ATT_TK, rows, diag=d)

    def normed(hh):
        acc = acc_ref[hh]
        l_lane = 0 if hh else HALF
        mine = jnp.logical_not(low) if hh else low
        l = jnp.sum(jnp.where(lane == l_lane, acc, 0.0), axis=-1, keepdims=True)
        ssq = jnp.sum(jnp.where(mine, acc * acc, 0.0), axis=-1, keepdims=True)
        inv_l = 1.0 / l
        return acc * (inv_l * lax.rsqrt(ssq * (inv_l * inv_l) * (1.0 / ATT_HEAD_DIM) + EPS))

    att = jnp.where(low, normed(0), normed(1)) * g_ref[...]
    o_ref[...] = (att * _silu(z_ref[...].astype(F32))).astype(o_ref.dtype)


def _skip_table(stats, bsz, seqlen):
    assert ATT_TK == IN_ROWS and ATT_TQ == 2 * ATT_TK
    nt = seqlen // ATT_TK
    nq = seqlen // ATT_TQ
    per_q = ATT_TQ // ATT_TK
    st = stats.reshape(bsz, nt, SUBLANES, LANES)
    qn = jnp.sqrt(st[:, :, 0, :ATT_HEADS]) * NORM_SLACK
    kn = jnp.sqrt(st[:, :, 1, :ATT_HEADS]) * NORM_SLACK
    c_first = st[:, :, 2, TAIL_F:TAIL_F + ATT_HEADS]
    c_last = st[:, :, 3, TAIL_F:TAIL_F + ATT_HEADS]
    by_q = lambda a: a.reshape(bsz, nq, per_q, ATT_HEADS)
    qmax = by_q(qn).max(axis=2)[:, :, None, :]
    kdiag = by_q(kn).max(axis=2)[:, :, None, :]
    decay = c_last[:, None, :, :] - by_q(c_first)[:, :, 0, None, :]
    key_tile = jnp.arange(nt, dtype=jnp.int32)[None, None, :, None]
    before = key_tile < (jnp.arange(nq, dtype=jnp.int32) * per_q)[None, :, None, None]
    dead = (qmax * (kn[:, None, :, :] + kdiag) + SKIP_MARGIN < decay) & before
    lead = jnp.min(jnp.where(dead, nt, key_tile), axis=2)
    lead = lead.reshape(bsz, nq, ATT_HEADS // 2, 2).min(axis=-1)
    return jnp.transpose(lead, (0, 2, 1)).reshape(-1).astype(jnp.int32)


def _fox_attn(qa, ka, va, z_att, g2, skip):
    bsz, _, seqlen, _ = qa.shape
    nq = seqlen // ATT_TQ
    q_spec = pl.BlockSpec((1, 2, ATT_TQ, LANES), lambda b, p, i, s: (b, p, i, 0))
    kv_spec = pl.BlockSpec((1, 2, seqlen, LANES), lambda b, p, i, s: (b, p, 0, 0))
    row_spec = pl.BlockSpec((ATT_TQ, LANES), lambda b, p, i, s: (b * nq + i, p))
    scratch = [((2, ATT_TQ, ATT_TK), F32), ((2, ATT_TQ, LANES), F32), ((2, ATT_TQ, LANES), F32)]
    vmem = _vmem_limit(
        pipelined=[((2, ATT_TQ, LANES), BF16), ((2 * 2, seqlen, LANES), BF16),
                   ((2 * ATT_TQ, LANES), BF16), _shape_dtype(g2)],
        resident=scratch,
        temporaries=[((2 * ATT_TQ, ATT_TK), F32), ((2 * ATT_TQ, ATT_TK), BF16),
                     ((4 * ATT_TQ, LANES), F32)])
    return pl.pallas_call(
        _fox_attn_kernel,
        out_shape=jax.ShapeDtypeStruct((bsz * seqlen, ATT_WIDTH), BF16),
        grid_spec=pltpu.PrefetchScalarGridSpec(
            num_scalar_prefetch=1,
            grid=(bsz, ATT_HEADS // 2, nq),
            in_specs=[q_spec, kv_spec, kv_spec, row_spec,
                      pl.BlockSpec(g2.shape, lambda b, p, i, s: (0, 0))],
            out_specs=row_spec,
            scratch_shapes=[pltpu.VMEM(*a) for a in scratch]),
        compiler_params=pltpu.CompilerParams(
            dimension_semantics=("arbitrary", "arbitrary", "arbitrary"),
            vmem_limit_bytes=vmem),
        name="fox_attn",
    )(skip, qa, ka, va, z_att, g2)


def _rms(h, g):
    ms = jnp.mean(h * h, axis=-1, keepdims=True)
    return h * lax.rsqrt(ms + EPS) * g


def _out_ple_kernel(x_ref, ys_ref, ya_ref, p_ref, wo_s_ref, wo_a_ref, wg_ref, wp_ref,
                    gp_ref, gf_ref, o_ref):
    h = x_ref[...]
    h = h + (jnp.dot(ys_ref[...], wo_s_ref[...], preferred_element_type=F32)
             + jnp.dot(ya_ref[...], wo_a_ref[...], preferred_element_type=F32))
    hn = _rms(h, gp_ref[...]).astype(BF16)
    gate = jax.nn.sigmoid(jnp.dot(hn, wg_ref[...], preferred_element_type=F32))
    emb = jnp.dot(p_ref[...].astype(BF16), wp_ref[...], preferred_element_type=F32)
    h = h + gate * emb
    o_ref[...] = _rms(h, gf_ref[...])


def _out_ple(x2, y_ssd, y_att, p2, wo_s, wo_a, wg, wp, gp, gf):
    n = x2.shape[0]
    row = lambda w: pl.BlockSpec((OUT_ROWS, w), lambda i: (i, 0))
    full = _const_spec
    resident = [wo_s, wo_a, wg, wp, gp, gf]
    vmem = _vmem_limit(
        pipelined=[((2 * OUT_ROWS, D_MODEL), F32), ((2 * OUT_ROWS, D_MODEL), BF16),
                   ((OUT_ROWS, PLE_DIM), F32)],
        resident=[_shape_dtype(a) for a in resident],
        temporaries=[((3 * OUT_ROWS, D_MODEL), F32), ((OUT_ROWS, D_MODEL), BF16)])
    return pl.pallas_call(
        _out_ple_kernel,
        out_shape=jax.ShapeDtypeStruct((n, D_MODEL), F32),
        grid=(n // OUT_ROWS,),
        in_specs=[row(D_MODEL), row(SSD_WIDTH), row(ATT_WIDTH), row(PLE_DIM)]
        + [full(a) for a in resident],
        out_specs=row(D_MODEL),
        compiler_params=pltpu.CompilerParams(
            dimension_semantics=("arbitrary",), vmem_limit_bytes=vmem),
        name="out_ple",
    )(x2, y_ssd, y_att, p2, *resident)


def _layer(h2, p2, bsz, seqlen, norm_g, w_in, conv_w, conv_b, dt_bias, a_log, d_skip,
           ssd_norm_g, fg_bias, att_norm_g, w_out, ple_norm_g, w_ple_gate, w_ple_proj, out_g):
    o_zs = 0
    o_dt = o_zs + SSD_WIDTH + CONV_CH
    o_za = o_dt + SSD_HEADS
    o_q = o_za + ATT_WIDTH
    o_k = o_q + ATT_WIDTH
    o_v = o_k + ATT_WIDTH
    o_f = o_v + ATT_WIDTH
    w_in16 = w_in.astype(BF16)
    order = jnp.argsort(fg_bias)
    fg_bias = fg_bias[order]

    def heads_cols(w):
        return w.reshape(D_MODEL, ATT_HEADS, ATT_HEAD_DIM)[:, order].reshape(D_MODEL, ATT_WIDTH)

    w_secs = [w_in16[:, o_zs:o_dt]] + [
        heads_cols(w_in16[:, o:o + ATT_WIDTH]) for o in (o_za, o_q, o_k, o_v)]
    w_f = w_in16[:, o_f:o_f + ATT_HEADS][:, order]
    w_tail = jnp.concatenate(
        [w_in16[:, o_dt:o_za]] + [w_f] * SPLIT_PARTS
        + [jnp.zeros((D_MODEL, LANES - SSD_HEADS - SPLIT_PARTS * ATT_HEADS), BF16)], axis=1)
    w_out_att = w_out[SSD_WIDTH:].astype(BF16).reshape(
        ATT_HEADS, ATT_HEAD_DIM, D_MODEL)[order].reshape(ATT_WIDTH, D_MODEL)

    def lanes16(v, off):
        return jnp.zeros((1, LANES), F32).at[0, off:off + v.shape[0]].set(v.astype(F32))

    z_ssd, xbc, z_att, qa, ka, va, tail, stats = _in_proj(
        h2, norm_g.reshape(1, D_MODEL).astype(F32), w_secs, w_tail,
        lanes16(jnp.tile(fg_bias, SPLIT_PARTS), TAIL_F), bsz, seqlen)

    y_ssd = _ssd(
        xbc, tail, z_ssd, conv_w.astype(F32), conv_b.reshape(1, CONV_CH).astype(F32),
        lanes16(dt_bias, TAIL_DT), lanes16(-jnp.exp(a_log.astype(F32)), TAIL_DT),
        jnp.repeat(d_skip.astype(F32), SSD_HEAD_DIM).reshape(1, SSD_WIDTH),
        ssd_norm_g.reshape(1, SSD_WIDTH).astype(F32), bsz, seqlen)

    y_att = _fox_attn(qa, ka, va, z_att,
                      jnp.tile(att_norm_g.astype(F32), 2).reshape(1, LANES),
                      _skip_table(stats, bsz, seqlen))

    return _out_ple(
        h2, y_ssd, y_att, p2,
        w_out[:SSD_WIDTH].astype(BF16), w_out_att,
        w_ple_gate.astype(BF16), w_ple_proj.astype(BF16),
        ple_norm_g.reshape(1, D_MODEL).astype(F32), out_g.reshape(1, D_MODEL).astype(F32))


def kernel(x, p, norm_g, w_in, conv_w, conv_b, dt_bias, a_log, d_skip, ssd_norm_g, fg_bias,
           att_norm_g, w_out, ple_norm_g, w_ple_gate, w_ple_proj, final_norm_g):
    bsz, seqlen, _ = x.shape
    depth = p.shape[0]
    assert depth == 1, "the fused tail applies the final norm right after the only layer"
    h2 = x.reshape(bsz * seqlen, D_MODEL)
    out = _layer(h2, p.reshape(bsz * seqlen, PLE_DIM), bsz, seqlen,
                 norm_g[0], w_in[0], conv_w[0], conv_b[0], dt_bias[0], a_log[0], d_skip[0],
                 ssd_norm_g[0], fg_bias[0], att_norm_g[0], w_out[0], ple_norm_g[0],
                 w_ple_gate[0], w_ple_proj[0], final_norm_g)
    return out.reshape(bsz, seqlen, D_MODEL)
```

```python
import jax
import jax.numpy as jnp
import numpy as np
from jax import lax
from jax.experimental import pallas as pl
from jax.experimental.pallas import tpu as pltpu

F32 = jnp.float32
BF16 = jnp.bfloat16

D_MODEL = 1024
SSD_WIDTH = 1024
ATT_WIDTH = 1024
SSD_HEAD_DIM = 64
SSD_HEADS = 16
N_GROUPS = 2
HEADS_PER_GROUP = SSD_HEADS // N_GROUPS
GROUP_WIDTH = SSD_WIDTH // N_GROUPS
D_STATE = 128
CONV_WIDTH = 4
CHUNK = 128
ATT_HEAD_DIM = 64
ATT_HEADS = 16
PLE_DIM = 256
EPS = 1e-6
CONV_CH = SSD_WIDTH + 2 * N_GROUPS * D_STATE

LANES = 128
SUBLANES = 8
HALF = LANES // 2
BF16_ROWS = 16
CONV_HALO = 128
V7X_VMEM_BYTES = 64 * 1024 * 1024

LOG2E = 1.4426950408889634
NEG_BIG = -1e30
SKIP_MARGIN = 136.0
NORM_SLACK = 1.01

SEC_ZS = (0, 0, SSD_WIDTH)
SEC_XBC = (0, SSD_WIDTH, CONV_CH)
SEC_ZA = (1, 0, ATT_WIDTH)
SEC_Q = (2, 0, ATT_WIDTH)
SEC_K = (3, 0, ATT_WIDTH)
SEC_V = (4, 0, ATT_WIDTH)
TAIL_DT = 0
TAIL_F = SSD_HEADS
SPLIT_PARTS = 3
TAIL_ONE = TAIL_DT

IN_ROWS = 512
IN_COL_CHUNK = 512
SSD_ROWS = 512
ATT_TQ = 1024
ATT_TK = 512
OUT_ROWS = 1024


def _nbytes(shape, dtype):
    return int(np.prod(shape)) * jnp.dtype(dtype).itemsize


def _vmem_limit(pipelined, resident, temporaries):
    need = (2 * sum(_nbytes(*a) for a in pipelined) + sum(_nbytes(*a) for a in resident)
            + sum(_nbytes(*a) for a in temporaries))
    assert need <= V7X_VMEM_BYTES, need
    return need


def _shape_dtype(a):
    return (a.shape, a.dtype)


def _split_bf16(x, parts):
    out = []
    r = x
    for i in range(parts):
        h = r.astype(BF16)
        out.append(h)
        if i + 1 < parts:
            r = r - h.astype(F32)
    return out


def _softplus(x):
    return jnp.maximum(x, 0.0) + jnp.log1p(jnp.exp(-jnp.abs(x)))


def _silu(x):
    return x * jax.nn.sigmoid(x)


def _aug_lane(head, j):
    pair, odd = divmod(head, 2)
    return pair * LANES + (j if odd else HALF + j)


def _aug_matrices():
    eq = np.zeros((LANES, ATT_WIDTH), np.float32)
    ek = np.zeros((LANES, ATT_WIDTH), np.float32)
    for h in range(ATT_HEADS):
        for part in range(SPLIT_PARTS):
            src = TAIL_F + part * ATT_HEADS + h
            eq[src, _aug_lane(h, part)] = 1.0
            ek[src, _aug_lane(h, SPLIT_PARTS + part)] = -1.0
            eq[TAIL_ONE, _aug_lane(h, SPLIT_PARTS + part)] = 1.0
            ek[TAIL_ONE, _aug_lane(h, part)] = 1.0
    return jnp.asarray(eq, BF16), jnp.asarray(ek, BF16)


def _tri3():
    tri = np.tril(np.ones((CHUNK, CHUNK), np.float32))
    return jnp.asarray(np.concatenate([tri, tri, tri], axis=1), BF16)


def _cumsum_rows(tri3, v):
    parts = _split_bf16(v, 3)
    return jnp.dot(tri3, jnp.concatenate(parts, axis=0), preferred_element_type=F32)


def _in_proj_kernel(x_ref, g_ref, ws_ref, wza_ref, wq_ref, wk_ref, wv_ref, wt_ref, fgb_ref,
                    tri3_ref, eq_ref, ek_ref,
                    zs_ref, xbc_ref, za_ref, qa_ref, ka_ref, va_ref, tail_ref, stats_ref,
                    carry_ref):
    w_refs = (ws_ref, wza_ref, wq_ref, wk_ref, wv_ref)

    @pl.when(pl.program_id(1) == 0)
    def _():
        carry_ref[...] = jnp.zeros_like(carry_ref)

    x = x_ref[...]
    ms = jnp.mean(x * x, axis=-1, keepdims=True)
    u = (x * lax.rsqrt(ms + EPS) * g_ref[...]).astype(BF16)

    def project(sec, c):
        return jnp.dot(u, w_refs[sec[0]][:, sec[1] + c:sec[1] + c + IN_COL_CHUNK],
                       preferred_element_type=F32)

    def section(out_ref, sec):
        for c in range(0, sec[2], IN_COL_CHUNK):
            out_ref[:, c:c + IN_COL_CHUNK] = project(sec, c).astype(out_ref.dtype)

    tail = jnp.dot(u, wt_ref[...], preferred_element_type=F32)
    tail_ref[...] = tail

    section(zs_ref, SEC_ZS)
    section(xbc_ref, SEC_XBC)
    section(za_ref, SEC_ZA)

    tri3 = tri3_ref[...]
    part_lane = lax.broadcasted_iota(jnp.int32, (CHUNK, LANES), 1)
    carry = carry_ref[...]
    lhs = []
    for c in range(IN_ROWS // CHUNK):
        log_f = -_softplus(-(tail[c * CHUNK:(c + 1) * CHUNK] + fgb_ref[...]))
        cum = _cumsum_rows(tri3, log_f) + carry
        carry = cum[CHUNK - 1:CHUNK, :]
        if c == 0:
            c_first = cum[0:1, :] * LOG2E
        hi, mid, lo = _split_bf16(cum * LOG2E, SPLIT_PARTS)
        parts = jnp.where(part_lane < TAIL_F + ATT_HEADS, hi,
                          jnp.where(part_lane < TAIL_F + 2 * ATT_HEADS, mid, lo))
        lhs.append(jnp.where(part_lane == TAIL_ONE, jnp.ones_like(parts), parts))
    carry_ref[...] = carry
    lhs = jnp.concatenate(lhs, axis=0)
    aq = jnp.dot(lhs, eq_ref[...], preferred_element_type=F32).astype(BF16)
    ak = jnp.dot(lhs, ek_ref[...], preferred_element_type=F32).astype(BF16)

    lane = lax.broadcasted_iota(jnp.int32, (IN_ROWS, LANES), 1)
    low = lane < HALF
    one_even = jnp.where(lane == HALF, 1.0, 0.0).astype(BF16)
    one_odd = jnp.where(lane == 0, 1.0, 0.0).astype(BF16)

    head_lane = lax.broadcasted_iota(jnp.int32, (1, LANES), 1)

    def head_major(out_ref, sec, aug_even, aug_odd, mul=None, norms=False):
        sq_max = jnp.zeros((1, LANES), F32)
        for c in range(0, sec[2], IN_COL_CHUNK):
            acc = project(sec, c)
            if mul is not None:
                acc = acc * mul
            acc16 = acc.astype(BF16)
            for i in range(IN_COL_CHUNK // LANES):
                pair = c // LANES + i
                d2 = acc16[:, i * LANES:(i + 1) * LANES]
                out_ref[0, 2 * pair] = jnp.where(low, d2, aug_even(pair))
                out_ref[0, 2 * pair + 1] = jnp.where(low, aug_odd(pair), d2)
                if norms:
                    sq = acc[:, i * LANES:(i + 1) * LANES]
                    sq = sq * sq
                    for odd in (0, 1):
                        mine = jnp.logical_not(low) if odd else low
                        rows = jnp.sum(jnp.where(mine, sq, 0.0), axis=-1, keepdims=True)
                        sq_max = jnp.where(head_lane == 2 * pair + odd,
                                           jnp.max(rows, axis=0, keepdims=True), sq_max)
        return sq_max

    pair_cols = lambda a: (lambda pair: a[:, pair * LANES:(pair + 1) * LANES])
    q_sq = head_major(qa_ref, SEC_Q, pair_cols(aq), pair_cols(aq),
                      mul=ATT_HEAD_DIM ** -0.5 * LOG2E, norms=True)
    k_sq = head_major(ka_ref, SEC_K, pair_cols(ak), pair_cols(ak), norms=True)
    head_major(va_ref, SEC_V, lambda pair: one_even, lambda pair: one_odd)
    stats_ref[0] = jnp.concatenate(
        [q_sq, k_sq, c_first, carry * LOG2E, jnp.zeros((SUBLANES - 4, LANES), F32)], axis=0)


def _const_spec(a):
    zeros = (0,) * a.ndim
    return pl.BlockSpec(a.shape, lambda *_: zeros, pipeline_mode=pl.Buffered(1))


def _in_proj(x2, g, w_secs, w_tail, fgb, bsz, seqlen):
    n = x2.shape[0]
    nt = seqlen // IN_ROWS
    tri3 = _tri3()
    eq, ek = _aug_matrices()
    row = lambda w: pl.BlockSpec((IN_ROWS, w), lambda b, t: (b * nt + t, 0))
    full = _const_spec
    head_major = pl.BlockSpec((1, ATT_HEADS, IN_ROWS, LANES), lambda b, t: (b, 0, t, 0))
    aug_shape = jax.ShapeDtypeStruct((bsz, ATT_HEADS, seqlen, LANES), BF16)
    out_shapes = (
        jax.ShapeDtypeStruct((n, SSD_WIDTH), BF16),
        jax.ShapeDtypeStruct((n, CONV_CH), BF16),
        jax.ShapeDtypeStruct((n, ATT_WIDTH), BF16),
        aug_shape, aug_shape, aug_shape,
        jax.ShapeDtypeStruct((n, LANES), F32),
        jax.ShapeDtypeStruct((bsz * nt, SUBLANES, LANES), F32),
    )
    resident = [g, *w_secs, w_tail, fgb, tri3, eq, ek]
    vmem = _vmem_limit(
        pipelined=[((IN_ROWS, D_MODEL), F32), ((IN_ROWS, SSD_WIDTH + CONV_CH + ATT_WIDTH), BF16),
                   ((3 * ATT_HEADS, IN_ROWS, LANES), BF16), ((IN_ROWS, LANES), F32)],
        resident=[_shape_dtype(a) for a in resident],
        temporaries=[((IN_ROWS, D_MODEL), BF16), ((2 * IN_ROWS, ATT_WIDTH), BF16),
                     ((8 * IN_ROWS, IN_COL_CHUNK), F32)])
    return pl.pallas_call(
        _in_proj_kernel,
        out_shape=out_shapes,
        grid=(bsz, nt),
        in_specs=[row(D_MODEL)] + [full(a) for a in resident],
        out_specs=(row(SSD_WIDTH), row(CONV_CH), row(ATT_WIDTH),
                   head_major, head_major, head_major, row(LANES),
                   pl.BlockSpec((1, SUBLANES, LANES), lambda b, t: (b * nt + t, 0, 0))),
        scratch_shapes=[pltpu.VMEM((1, LANES), F32)],
        compiler_params=pltpu.CompilerParams(
            dimension_semantics=("arbitrary", "arbitrary"), vmem_limit_bytes=vmem),
        name="in_proj",
    )(x2, *resident)


def _head_expand2():
    e = np.zeros((2 * LANES, SSD_WIDTH), np.float32)
    for h in range(SSD_HEADS):
        e[TAIL_DT + h, h * SSD_HEAD_DIM:(h + 1) * SSD_HEAD_DIM] = 1.0
        e[LANES + TAIL_DT + h, h * SSD_HEAD_DIM:(h + 1) * SSD_HEAD_DIM] = 1.0
    return jnp.asarray(e, BF16)


def _conv_shift():
    s = np.zeros((CONV_WIDTH * CHUNK, CONV_HALO + CHUNK), np.float32)
    for j in range(CONV_WIDTH):
        for l in range(CHUNK):
            s[j * CHUNK + l, CONV_HALO + l - (CONV_WIDTH - 1) + j] = 1.0
    return jnp.asarray(s, BF16)


def _ssd_kernel(xbc_ref, tail_ref, z_ref, cw_ref, cb_ref, dtb_ref, a_ref, dfull_ref, g_ref,
                tri3_ref, e2_ref, shift_ref, xprev_ref, y_ref, xwin_ref, state_ref):
    @pl.when(pl.program_id(1) == 0)
    def _():
        state_ref[...] = jnp.zeros_like(state_ref)

    prev = xprev_ref[...]
    xwin_ref[0:CONV_HALO, :] = jnp.where(pl.program_id(1) > 0, prev, jnp.zeros_like(prev))
    xwin_ref[CONV_HALO:CONV_HALO + SSD_ROWS, :] = xbc_ref[...]

    tri3 = tri3_ref[...]
    row_i = lax.broadcasted_iota(jnp.int32, (CHUNK, CHUNK), 0)
    col_i = lax.broadcasted_iota(jnp.int32, (CHUNK, CHUNK), 1)
    causal = row_i >= col_i
    low = col_i < HALF

    for c in range(SSD_ROWS // CHUNK):
        r0 = c * CHUNK
        window = xwin_ref[r0:r0 + CONV_HALO + CHUNK, :]
        taps = jnp.dot(shift_ref[...], window, preferred_element_type=F32)
        conv = cb_ref[...]
        for j in range(CONV_WIDTH):
            conv = conv + cw_ref[j:j + 1, :] * taps[j * CHUNK:(j + 1) * CHUNK]
        xc = _silu(conv)
        xs = xc[:, :SSD_WIDTH]
        bm = xc[:, SSD_WIDTH:SSD_WIDTH + N_GROUPS * D_STATE].astype(BF16)
        cm = xc[:, SSD_WIDTH + N_GROUPS * D_STATE:].astype(BF16)

        dt = _softplus(tail_ref[r0:r0 + CHUNK, :] + dtb_ref[...])
        adt = dt * a_ref[...]
        acs = _cumsum_rows(tri3, adt)
        acs_t = acs.T
        last = acs[CHUNK - 1:CHUNK, :]
        w = dt * jnp.exp(last - acs)
        ea = jnp.exp(acs)
        cd = jnp.broadcast_to(jnp.exp(last), (BF16_ROWS, LANES))
        stack = jnp.concatenate([dt, w, ea, cd], axis=0)
        lhs = jnp.concatenate(_split_bf16(stack, 2), axis=1)
        spread = jnp.dot(lhs, e2_ref[...], preferred_element_type=F32)
        dt_full = spread[0:CHUNK]
        w_full = spread[CHUNK:2 * CHUNK]
        ea_full = spread[2 * CHUNK:3 * CHUNK]
        cd_full = spread[3 * CHUNK:3 * CHUNK + 1]

        xdt = (xs * dt_full).astype(BF16)
        wx = (xs * w_full).astype(BF16)

        y_blocks = []
        yoff_blocks = []
        for g in range(N_GROUPS):
            gs = slice(g * D_STATE, (g + 1) * D_STATE)
            cb_g = lax.dot_general(cm[:, gs], bm[:, gs], (((1,), (1,)), ((), ())),
                                   preferred_element_type=F32)
            for jp in range(HEADS_PER_GROUP // 2):
                pair = g * (HEADS_PER_GROUP // 2) + jp
                x_pair = xdt[:, pair * LANES:(pair + 1) * LANES]
                ys = []
                for r in (2 * pair, 2 * pair + 1):
                    seg = acs[:, r:r + 1] - acs_t[r:r + 1, :]
                    dec = jnp.exp(jnp.where(causal, seg, -jnp.inf))
                    m = (cb_g * dec).astype(BF16)
                    ys.append(jnp.dot(m, x_pair, preferred_element_type=F32))
                y_blocks.append(jnp.where(low, ys[0], ys[1]))
            hs = slice(g * GROUP_WIDTH, (g + 1) * GROUP_WIDTH)
            st = state_ref[:, hs]
            yoff_blocks.append(jnp.dot(cm[:, gs], st.astype(BF16), preferred_element_type=F32))
            s_g = lax.dot_general(bm[:, gs], wx[:, hs], (((0,), (0,)), ((), ())),
                                  preferred_element_type=F32)
            state_ref[:, hs] = st * cd_full[:, hs] + s_g
        y = jnp.concatenate(y_blocks, axis=1)
        y_off = jnp.concatenate(yoff_blocks, axis=1)
        y = y + y_off * ea_full + dfull_ref[...] * xs
        y = y * _silu(z_ref[r0:r0 + CHUNK, :].astype(F32))
        outs = []
        for g in range(N_GROUPS):
            yg = y[:, g * GROUP_WIDTH:(g + 1) * GROUP_WIDTH]
            ms = jnp.mean(yg * yg, axis=-1, keepdims=True)
            outs.append(yg * lax.rsqrt(ms + EPS))
        yn = jnp.concatenate(outs, axis=1) * g_ref[...]
        y_ref[r0:r0 + CHUNK, :] = yn.astype(y_ref.dtype)


def _ssd(xbc, tail, z_ssd, conv_w, conv_b, dtb, a_neg, d_full, g, bsz, seqlen):
    nt = seqlen // SSD_ROWS
    tri3 = _tri3()
    e2 = _head_expand2()
    shift = _conv_shift()
    row = lambda w: pl.BlockSpec((SSD_ROWS, w), lambda b, t: (b * nt + t, 0))
    full = _const_spec
    halo_per_tile = SSD_ROWS // CONV_HALO
    prev = pl.BlockSpec(
        (CONV_HALO, CONV_CH),
        lambda b, t: (jnp.maximum((b * nt + t) * halo_per_tile - 1, 0), 0))
    resident = [conv_w, conv_b, dtb, a_neg, d_full, g, tri3, e2, shift]
    xwin = ((CONV_HALO + SSD_ROWS, CONV_CH), BF16)
    state = ((D_STATE, SSD_WIDTH), F32)
    vmem = _vmem_limit(
        pipelined=[((SSD_ROWS + CONV_HALO, CONV_CH), BF16), ((SSD_ROWS, LANES), F32),
                   ((2 * SSD_ROWS, SSD_WIDTH), BF16)],
        resident=[_shape_dtype(a) for a in resident] + [xwin, state],
        temporaries=[((CONV_WIDTH * CHUNK + 2 * CHUNK, CONV_CH), F32),
                     ((400 + 6 * CHUNK, SSD_WIDTH), F32)] * (SSD_ROWS // CHUNK))
    return pl.pallas_call(
        _ssd_kernel,
        out_shape=jax.ShapeDtypeStruct((bsz * seqlen, SSD_WIDTH), BF16),
        grid=(bsz, nt),
        in_specs=[row(CONV_CH), row(LANES), row(SSD_WIDTH)] + [full(a) for a in resident]
        + [prev],
        out_specs=row(SSD_WIDTH),
        scratch_shapes=[pltpu.VMEM(*xwin), pltpu.VMEM(*state)],
        compiler_params=pltpu.CompilerParams(
            dimension_semantics=("arbitrary", "arbitrary"), vmem_limit_bytes=vmem),
        name="ssd",
    )(xbc, tail, z_ssd, *resident, xbc)


def _fox_attn_kernel(skip_ref, qa_ref, ka_ref, va_ref, z_ref, g_ref, o_ref, s_ref, acc_ref,
                     m_ref):
    qi = pl.program_id(2)
    lane = lax.broadcasted_iota(jnp.int32, (ATT_TQ, LANES), 1)
    low = lane < HALF
    nt_dims = (((1,), (1,)), ((), ()))
    n_diag = ATT_TQ // ATT_TK
    all_rows = (0, ATT_TQ)

    def scores(hh, kstart, rows=all_rows):
        k = ka_ref[0, hh, pl.ds(kstart, ATT_TK), :]
        s_ref[hh, rows[0]:rows[1], :] = lax.dot_general(
            qa_ref[0, hh, rows[0]:rows[1], :], k, nt_dims, preferred_element_type=F32)

    def softmax_pv(hh, kstart, rows=all_rows, diag=None):
        r0, r1 = rows
        s = s_ref[hh, r0:r1, :]
        if diag is not None:
            row_i = lax.broadcasted_iota(jnp.int32, (r1 - r0, ATT_TK), 0) + r0
            col_i = lax.broadcasted_iota(jnp.int32, (r1 - r0, ATT_TK), 1) + diag * ATT_TK
            s = jnp.where(row_i >= col_i, s, -jnp.inf)
        m_old = m_ref[hh, r0:r1, :]
        m_new = jnp.maximum(m_old, jnp.max(s, axis=-1, keepdims=True))
        p = jnp.exp2(s - jnp.tile(m_new, (1, ATT_TK // LANES))).astype(BF16)
        alpha = jnp.exp2(m_old - m_new)
        m_ref[hh, r0:r1, :] = m_new
        v = va_ref[0, hh, pl.ds(kstart, ATT_TK), :]
        acc_ref[hh, r0:r1, :] = (alpha * acc_ref[hh, r0:r1, :]
                                 + jnp.dot(p, v, preferred_element_type=F32))

    first = skip_ref[(pl.program_id(0) * pl.num_programs(1) + pl.program_id(1))
                     * pl.num_programs(2) + qi]
    m_ref[...] = jnp.full(m_ref.shape, NEG_BIG, F32)
    acc_ref[...] = jnp.zeros(acc_ref.shape, F32)
    scores(0, pl.multiple_of(first * ATT_TK, ATT_TK))

    def key_tile_step(kstart):
        scores(1, kstart)
        softmax_pv(0, kstart)
        scores(0, kstart + ATT_TK)
        softmax_pv(1, kstart)

    @pl.when(first % 2 == 1)
    def _():
        key_tile_step(pl.multiple_of(first * ATT_TK, ATT_TK))

    def two_tiles(trip):
        for u in range(n_diag):
            key_tile_step(pl.multiple_of(trip * ATT_TQ, ATT_TQ) + u * ATT_TK)

    trip0 = (first + 1) // 2
    n_trips = qi - trip0

    @pl.when(n_trips % 2 == 1)
    def _():
        two_tiles(trip0)

    trip1 = trip0 + n_trips % 2

    def body(j, carry):
        two_tiles(trip1 + 2 * j)
        two_tiles(trip1 + 2 * j + 1)
        return carry

    lax.fori_loop(0, n_trips // 2, body, 0)
    kdiag = pl.multiple_of(qi * ATT_TQ, ATT_TQ)
    for d in range(n_diag):
        rows = (d * ATT_TK, ATT_TQ)
        scores(1, kdiag + d * ATT_TK, rows)
        softmax_pv(0, kdiag + d * ATT_TK, rows, diag=d)
        if d + 1 < n_diag:
            scores(0, kdiag + (d + 1) * ATT_TK, ((d + 1) * ATT_TK, ATT_TQ))
        softmax_pv(1, kdiag + d * ATT_TK, rows, diag=d)

    def normed(hh):
        acc = acc_ref[hh]
        l_lane = 0 if hh else HALF
        mine = jnp.logical_not(low) if hh else low
        l = jnp.sum(jnp.where(lane == l_lane, acc, 0.0), axis=-1, keepdims=True)
        ssq = jnp.sum(jnp.where(mine, acc * acc, 0.0), axis=-1, keepdims=True)
        inv_l = 1.0 / l
        return acc * (inv_l * lax.rsqrt(ssq * (inv_l * inv_l) * (1.0 / ATT_HEAD_DIM) + EPS))

    att = jnp.where(low, normed(0), normed(1)) * g_ref[...]
    o_ref[...] = (att * _silu(z_ref[...].astype(F32))).astype(o_ref.dtype)


def _skip_table(stats, bsz, seqlen):
    assert ATT_TK == IN_ROWS and ATT_TQ == 2 * ATT_TK
    nt = seqlen // ATT_TK
    nq = seqlen // ATT_TQ
    per_q = ATT_TQ // ATT_TK
    st = stats.reshape(bsz, nt, SUBLANES, LANES)
    qn = jnp.sqrt(st[:, :, 0, :ATT_HEADS]) * NORM_SLACK
    kn = jnp.sqrt(st[:, :, 1, :ATT_HEADS]) * NORM_SLACK
    c_first = st[:, :, 2, TAIL_F:TAIL_F + ATT_HEADS]
    c_last = st[:, :, 3, TAIL_F:TAIL_F + ATT_HEADS]
    by_q = lambda a: a.reshape(bsz, nq, per_q, ATT_HEADS)
    qmax = by_q(qn).max(axis=2)[:, :, None, :]
    kdiag = by_q(kn).max(axis=2)[:, :, None, :]
    decay = c_last[:, None, :, :] - by_q(c_first)[:, :, 0, None, :]
    key_tile = jnp.arange(nt, dtype=jnp.int32)[None, None, :, None]
    before = key_tile < (jnp.arange(nq, dtype=jnp.int32) * per_q)[None, :, None, None]
    dead = (qmax * (kn[:, None, :, :] + kdiag) + SKIP_MARGIN < decay) & before
    lead = jnp.min(jnp.where(dead, nt, key_tile), axis=2)
    lead = lead.reshape(bsz, nq, ATT_HEADS // 2, 2).min(axis=-1)
    return jnp.transpose(lead, (0, 2, 1)).reshape(-1).astype(jnp.int32)


def _fox_attn(qa, ka, va, z_att, g2, skip):
    bsz, _, seqlen, _ = qa.shape
    nq = seqlen // ATT_TQ
    q_spec = pl.BlockSpec((1, 2, ATT_TQ, LANES), lambda b, p, i, s: (b, p, i, 0))
    kv_spec = pl.BlockSpec((1, 2, seqlen, LANES), lambda b, p, i, s: (b, p, 0, 0))
    row_spec = pl.BlockSpec((ATT_TQ, LANES), lambda b, p, i, s: (b * nq + i, p))
    scratch = [((2, ATT_TQ, ATT_TK), F32), ((2, ATT_TQ, LANES), F32), ((2, ATT_TQ, LANES), F32)]
    vmem = _vmem_limit(
        pipelined=[((2, ATT_TQ, LANES), BF16), ((2 * 2, seqlen, LANES), BF16),
                   ((2 * ATT_TQ, LANES), BF16), _shape_dtype(g2)],
        resident=scratch,
        temporaries=[((2 * ATT_TQ, ATT_TK), F32), ((2 * ATT_TQ, ATT_TK), BF16),
                     ((4 * ATT_TQ, LANES), F32)])
    return pl.pallas_call(
        _fox_attn_kernel,
        out_shape=jax.ShapeDtypeStruct((bsz * seqlen, ATT_WIDTH), BF16),
        grid_spec=pltpu.PrefetchScalarGridSpec(
            num_scalar_prefetch=1,
            grid=(bsz, ATT_HEADS // 2, nq),
            in_specs=[q_spec, kv_spec, kv_spec, row_spec,
                      pl.BlockSpec(g2.shape, lambda b, p, i, s: (0, 0))],
            out_specs=row_spec,
            scratch_shapes=[pltpu.VMEM(*a) for a in scratch]),
        compiler_params=pltpu.CompilerParams(
            dimension_semantics=("arbitrary", "arbitrary", "arbitrary"),
            vmem_limit_bytes=vmem),
        name="fox_attn",
    )(skip, qa, ka, va, z_att, g2)


def _rms(h, g):
    ms = jnp.mean(h * h, axis=-1, keepdims=True)
    return h * lax.rsqrt(ms + EPS) * g


def _out_ple_kernel(x_ref, ys_ref, ya_ref, p_ref, wo_s_ref, wo_a_ref, wg_ref, wp_ref,
                    gp_ref, gf_ref, o_ref):
    h = x_ref[...]
    h = h + (jnp.dot(ys_ref[...], wo_s_ref[...], preferred_element_type=F32)
             + jnp.dot(ya_ref[...], wo_a_ref[...], preferred_element_type=F32))
    emb = jnp.dot(p_ref[...].astype(BF16), wp_ref[...], preferred_element_type=F32)
    hn = _rms(h, gp_ref[...]).astype(BF16)
    gate = jax.nn.sigmoid(jnp.dot(hn, wg_ref[...], preferred_element_type=F32))
    h = h + gate * emb
    o_ref[...] = _rms(h, gf_ref[...])


def _out_ple(x2, y_ssd, y_att, p2, wo_s, wo_a, wg, wp, gp, gf):
    n = x2.shape[0]
    row = lambda w: pl.BlockSpec((OUT_ROWS, w), lambda i: (i, 0))
    full = _const_spec
    resident = [wo_s, wo_a, wg, wp, gp, gf]
    vmem = _vmem_limit(
        pipelined=[((2 * OUT_ROWS, D_MODEL), F32), ((2 * OUT_ROWS, D_MODEL), BF16),
                   ((OUT_ROWS, PLE_DIM), F32)],
        resident=[_shape_dtype(a) for a in resident],
        temporaries=[((3 * OUT_ROWS, D_MODEL), F32), ((OUT_ROWS, D_MODEL), BF16)])
    return pl.pallas_call(
        _out_ple_kernel,
        out_shape=jax.ShapeDtypeStruct((n, D_MODEL), F32),
        grid=(n // OUT_ROWS,),
        in_specs=[row(D_MODEL), row(SSD_WIDTH), row(ATT_WIDTH), row(PLE_DIM)]
        + [full(a) for a in resident],
        out_specs=row(D_MODEL),
        compiler_params=pltpu.CompilerParams(
            dimension_semantics=("arbitrary",), vmem_limit_bytes=vmem),
        name="out_ple",
    )(x2, y_ssd, y_att, p2, *resident)


def _layer(h2, p2, bsz, seqlen, norm_g, w_in, conv_w, conv_b, dt_bias, a_log, d_skip,
           ssd_norm_g, fg_bias, att_norm_g, w_out, ple_norm_g, w_ple_gate, w_ple_proj, out_g):
    o_zs = 0
    o_dt = o_zs + SSD_WIDTH + CONV_CH
    o_za = o_dt + SSD_HEADS
    o_q = o_za + ATT_WIDTH
    o_k = o_q + ATT_WIDTH
    o_v = o_k + ATT_WIDTH
    o_f = o_v + ATT_WIDTH
    w_in16 = w_in.astype(BF16)
    order = jnp.argsort(fg_bias)
    fg_bias = fg_bias[order]

    def heads_cols(w):
        return w.reshape(D_MODEL, ATT_HEADS, ATT_HEAD_DIM)[:, order].reshape(D_MODEL, ATT_WIDTH)

    w_secs = [w_in16[:, o_zs:o_dt]] + [
        heads_cols(w_in16[:, o:o + ATT_WIDTH]) for o in (o_za, o_q, o_k, o_v)]
    w_f = w_in16[:, o_f:o_f + ATT_HEADS][:, order]
    w_tail = jnp.concatenate(
        [w_in16[:, o_dt:o_za]] + [w_f] * SPLIT_PARTS
        + [jnp.zeros((D_MODEL, LANES - SSD_HEADS - SPLIT_PARTS * ATT_HEADS), BF16)], axis=1)
    w_out_att = w_out[SSD_WIDTH:].astype(BF16).reshape(
        ATT_HEADS, ATT_HEAD_DIM, D_MODEL)[order].reshape(ATT_WIDTH, D_MODEL)

    def lanes16(v, off):
        return jnp.zeros((1, LANES), F32).at[0, off:off + v.shape[0]].set(v.astype(F32))

    z_ssd, xbc, z_att, qa, ka, va, tail, stats = _in_proj(
        h2, norm_g.reshape(1, D_MODEL).astype(F32), w_secs, w_tail,
        lanes16(jnp.tile(fg_bias, SPLIT_PARTS), TAIL_F), bsz, seqlen)

    y_ssd = _ssd(
        xbc, tail, z_ssd, conv_w.astype(F32), conv_b.reshape(1, CONV_CH).astype(F32),
        lanes16(dt_bias, TAIL_DT), lanes16(-jnp.exp(a_log.astype(F32)), TAIL_DT),
        jnp.repeat(d_skip.astype(F32), SSD_HEAD_DIM).reshape(1, SSD_WIDTH),
        ssd_norm_g.reshape(1, SSD_WIDTH).astype(F32), bsz, seqlen)

    y_att = _fox_attn(qa, ka, va, z_att,
                      jnp.tile(att_norm_g.astype(F32), 2).reshape(1, LANES),
                      _skip_table(stats, bsz, seqlen))

    return _out_ple(
        h2, y_ssd, y_att, p2,
        w_out[:SSD_WIDTH].astype(BF16), w_out_att,
        w_ple_gate.astype(BF16), w_ple_proj.astype(BF16),
        ple_norm_g.reshape(1, D_MODEL).astype(F32), out_g.reshape(1, D_MODEL).astype(F32))


def kernel(x, p, norm_g, w_in, conv_w, conv_b, dt_bias, a_log, d_skip, ssd_norm_g, fg_bias,
           att_norm_g, w_out, ple_norm_g, w_ple_gate, w_ple_proj, final_norm_g):
    bsz, seqlen, _ = x.shape
    depth = p.shape[0]
    assert depth == 1, "the fused tail applies the final norm right after the only layer"
    h2 = x.reshape(bsz * seqlen, D_MODEL)
    out = _layer(h2, p.reshape(bsz * seqlen, PLE_DIM), bsz, seqlen,
                 norm_g[0], w_in[0], conv_w[0], conv_b[0], dt_bias[0], a_log[0], d_skip[0],
                 ssd_norm_g[0], fg_bias[0], att_norm_g[0], w_out[0], ple_norm_g[0],
                 w_ple_gate[0], w_ple_proj[0], final_norm_g)
    return out.reshape(bsz, seqlen, D_MODEL)
```
